```python
import math
import jax, jax.numpy as jnp
from jax import lax
import numpy as np

D_MODEL = 1024
BATCH = 2
SEQ = 16384
DEPTH = 2

CONV_CH = D_MODEL // 4
CONV_GROUPS = 4
CONV_WIDTH = 31
RET_VDIM = 64
RET_KDIM = RET_VDIM // 2
RET_HEADS = (3 * D_MODEL // 8) // RET_VDIM
RET_CHUNK = 128
ATT_HDIM = 64
ATT_HEADS = (3 * D_MODEL // 8) // ATT_HDIM
DILATED_BRANCHES = ((128, 1), (512, 4), (2048, 16))
ATT_BLOCK = 128
N_BUCKETS = 32
MAX_DISTANCE = 2048
D_FF = 4 * D_MODEL
EPS = 1e-6
ROPE_BASE = 10000.0
NEG_INF = -1e30

RET_W = RET_HEADS * RET_VDIM
ATT_W = ATT_HEADS * ATT_HDIM
MIX_W = CONV_CH + RET_W + ATT_W
IN_SIZES = (2 * CONV_CH, RET_HEADS * RET_KDIM, RET_HEADS * RET_KDIM, RET_W, RET_W, ATT_W, ATT_W, ATT_W)
IN_W = sum(IN_SIZES)
IN_SPLITS = [sum(IN_SIZES[:i + 1]) for i in range(len(IN_SIZES) - 1)]

kernel_name = "hybrid_conv_retention_dilated_attn"

F32 = jnp.float32


def rms_norm(x, g):
    xf = x.astype(F32)
    return xf * lax.rsqrt(jnp.mean(xf * xf, axis=-1, keepdims=True) + EPS) * g.astype(F32)


def conv_mixer(u, conv_w, conv_b, conv_g):
    B, S, _ = u.shape
    a, gate = jnp.split(u, 2, axis=-1)
    h = a * jax.nn.sigmoid(gate)
    h = lax.conv_general_dilated(
        h, conv_w.astype(F32)[:, None, :], window_strides=(1,),
        padding=[(CONV_WIDTH - 1, 0)],
        dimension_numbers=("NWC", "WIO", "NWC"),
        feature_group_count=CONV_CH) + conv_b.astype(F32)
    hg = h.reshape(B, S, CONV_GROUPS, CONV_CH // CONV_GROUPS)
    hg = hg * lax.rsqrt(jnp.mean(hg * hg, axis=-1, keepdims=True) + EPS)
    h = hg.reshape(B, S, CONV_CH) * conv_g.astype(F32)
    return jax.nn.silu(h)


def rotary(x, pos):
    d = x.shape[-1]
    inv = 1.0 / (ROPE_BASE ** jnp.linspace(0.0, 1.0, d // 2, dtype=F32))
    ang = pos[:, None] * inv[None, :]
    c, s = jnp.cos(ang)[:, None, :], jnp.sin(ang)[:, None, :]
    x1, x2 = x[..., 0::2], x[..., 1::2]
    return jnp.stack([x1 * c - x2 * s, x1 * s + x2 * c], axis=-1).reshape(x.shape)


def retention(q, k, v, g, ret_g):
    B, S, H, dk = q.shape
    dv = v.shape[-1]
    C = RET_CHUNK
    N = S // C
    pos = jnp.arange(S, dtype=F32)
    q = rotary(q, pos)
    k = rotary(k, pos) * (dk ** -0.5)
    to_chunks = lambda t: t.reshape(B, N, C, H, t.shape[-1]).transpose(0, 3, 1, 2, 4)
    qc, kc, vc = to_chunks(q), to_chunks(k), to_chunks(v)
    log_gamma = jnp.log(1.0 - 2.0 ** (-5.0 - jnp.arange(H, dtype=F32)))
    idx = jnp.arange(C, dtype=F32)
    diff = idx[:, None] - idx[None, :]
    decay = jnp.where(diff >= 0, jnp.exp(log_gamma[:, None, None] * jnp.maximum(diff, 0.0)), 0.0)
    scores = jnp.einsum("bhnid,bhnjd->bhnij", qc, kc) * decay[None, :, None]
    inner = jnp.einsum("bhnij,bhnje->bhnie", scores, vc)
    k_dec = kc * jnp.exp(log_gamma[:, None] * (C - 1.0 - idx))[None, :, None, :, None]
    kv = jnp.einsum("bhnjd,bhnje->bhnde", k_dec, vc)
    chunk_decay = jnp.exp(log_gamma * C)[None, :, None, None]

    def step(state, kv_n):
        return state * chunk_decay + kv_n, state

    _, states = lax.scan(step, jnp.zeros((B, H, dk, dv), F32), jnp.moveaxis(kv, 2, 0))
    states = jnp.moveaxis(states, 0, 2)
    q_dec = qc * jnp.exp(log_gamma[:, None] * (idx + 1.0))[None, :, None, :, None]
    cross = jnp.einsum("bhnid,bhnde->bhnie", q_dec, states)
    o = (inner + cross).transpose(0, 2, 3, 1, 4).reshape(B, S, H, dv)
    o = o * lax.rsqrt(jnp.mean(o * o, axis=-1, keepdims=True) + EPS)
    o = o.reshape(B, S, H * dv) * ret_g.astype(F32)
    return o * jax.nn.silu(g)


def t5_bucket(dist):
    max_exact = N_BUCKETS // 2
    nf = jnp.maximum(dist, 1).astype(F32)
    large = max_exact + (jnp.log(nf / max_exact) / math.log(MAX_DISTANCE / max_exact)
                         * (N_BUCKETS - max_exact)).astype(jnp.int32)
    large = jnp.minimum(large, N_BUCKETS - 1)
    return jnp.where(dist < max_exact, dist, large)


def dilated_branch(q, k, v, rel_bias, dilation):
    B, H, S, hd = q.shape
    Lb = ATT_BLOCK
    span = Lb * dilation
    S_pad = -(-S // span) * span
    L = S_pad // dilation
    nb = L // Lb

    def gather(t):
        t = jnp.pad(t, ((0, 0), (0, 0), (0, S_pad - S), (0, 0)))
        return t.reshape(B, H, L, dilation, hd).transpose(0, 1, 3, 2, 4).reshape(B, H, dilation, nb, Lb, hd)

    def with_prev(t):
        prev = jnp.concatenate([jnp.zeros_like(t[:, :, :, :1]), t[:, :, :, :-1]], axis=3)
        return jnp.concatenate([prev, t], axis=4)

    qg = gather(q)
    kk, vv = with_prev(gather(k)), with_prev(gather(v))
    qi = jnp.arange(Lb, dtype=jnp.int32)[:, None]
    kj = jnp.arange(2 * Lb, dtype=jnp.int32)[None, :]
    dist = Lb + qi - kj
    band = (dist >= 0) & (dist <= Lb)
    valid = band[None] & ((jnp.arange(nb)[:, None, None] > 0) | (kj[None] >= Lb))
    bias = rel_bias.astype(F32)[t5_bucket(jnp.clip(dist, 0, Lb) * dilation)]
    bias = bias.transpose(2, 0, 1)[None, :, None, None]
    logits = jnp.einsum("bhgnqd,bhgnkd->bhgnqk", qg, kk) * (hd ** -0.5) + bias
    logits = jnp.where(valid, logits, NEG_INF)
    m = jnp.max(logits, axis=-1, keepdims=True)
    p = jnp.exp(logits - m)
    s = jnp.sum(p, axis=-1, keepdims=True)
    o = jnp.einsum("bhgnqk,bhgnkd->bhgnqd", p, vv) / s
    lse = (m + jnp.log(s))[..., 0]
    o = o.reshape(B, H, dilation, L, hd).transpose(0, 1, 3, 2, 4).reshape(B, H, S_pad, hd)[:, :, :S]
    lse = lse.reshape(B, H, dilation, L).transpose(0, 1, 3, 2).reshape(B, H, S_pad)[:, :, :S]
    return o, lse


def dilated_attention(q, k, v, q_g, k_g, rel_bias):
    B, S, H, hd = q.shape
    q = rms_norm(q, q_g).transpose(0, 2, 1, 3)
    k = rms_norm(k, k_g).transpose(0, 2, 1, 3)
    v = v.transpose(0, 2, 1, 3)
    outs, lses = [], []
    for _, dilation in DILATED_BRANCHES:
        o, lse = dilated_branch(q, k, v, rel_bias, dilation)
        outs.append(o)
        lses.append(lse)
    w = jax.nn.softmax(jnp.stack(lses, axis=0), axis=0)
    o = jnp.sum(w[..., None] * jnp.stack(outs, axis=0), axis=0)
    return o.transpose(0, 2, 1, 3).reshape(B, S, H * hd)


def hybrid_layer(x, norm1_g, w_in, conv_w, conv_b, conv_g, ret_g, q_g, k_g,
                 w_out, norm2_g, w_ff1, w_ff2, rel_bias):
    B, S, _ = x.shape
    h = rms_norm(x, norm1_g)
    u = h @ w_in.astype(F32)
    u_conv, rq, rk, rv, rg, aq, ak, av = jnp.split(u, IN_SPLITS, axis=-1)
    conv_out = conv_mixer(u_conv, conv_w, conv_b, conv_g)
    ret_out = retention(rq.reshape(B, S, RET_HEADS, RET_KDIM), rk.reshape(B, S, RET_HEADS, RET_KDIM),
                        rv.reshape(B, S, RET_HEADS, RET_VDIM), rg, ret_g)
    att_out = dilated_attention(aq.reshape(B, S, ATT_HEADS, ATT_HDIM), ak.reshape(B, S, ATT_HEADS, ATT_HDIM),
                                av.reshape(B, S, ATT_HEADS, ATT_HDIM), q_g, k_g, rel_bias)
    mix = jnp.concatenate([conv_out, ret_out, att_out], axis=-1) @ w_out.astype(F32)
    x = x + mix.astype(x.dtype)
    h2 = rms_norm(x, norm2_g)
    ff = jnp.square(jax.nn.relu(h2 @ w_ff1.astype(F32))) @ w_ff2.astype(F32)
    return x + ff.astype(x.dtype)


def setup_inputs(seed: int = 0) -> dict:
    key = jax.random.key(seed)
    ks = jax.random.split(key, 15)
    nrm = lambda k, shape: jax.random.normal(k, shape, F32)
    return {
        "x": nrm(ks[0], (BATCH, SEQ, D_MODEL)),
        "norm1_g": 1.0 + 0.02 * nrm(ks[1], (DEPTH, D_MODEL)),
        "w_in": nrm(ks[2], (DEPTH, D_MODEL, IN_W)) * D_MODEL ** -0.5,
        "conv_w": nrm(ks[3], (DEPTH, CONV_WIDTH, CONV_CH)) * CONV_WIDTH ** -0.5,
        "conv_b": 0.02 * nrm(ks[4], (DEPTH, CONV_CH)),
        "conv_g": 1.0 + 0.02 * nrm(ks[5], (DEPTH, CONV_CH)),
        "ret_g": 1.0 + 0.02 * nrm(ks[6], (DEPTH, RET_W)),
        "q_g": 1.0 + 0.02 * nrm(ks[7], (DEPTH, ATT_HDIM)),
        "k_g": 1.0 + 0.02 * nrm(ks[8], (DEPTH, ATT_HDIM)),
        "w_out": nrm(ks[9], (DEPTH, MIX_W, D_MODEL)) * MIX_W ** -0.5,
        "norm2_g": 1.0 + 0.02 * nrm(ks[10], (DEPTH, D_MODEL)),
        "w_ff1": nrm(ks[11], (DEPTH, D_MODEL, D_FF)) * D_MODEL ** -0.5,
        "w_ff2": nrm(ks[12], (DEPTH, D_FF, D_MODEL)) * D_FF ** -0.5,
        "rel_bias": 0.1 * nrm(ks[13], (N_BUCKETS, ATT_HEADS)),
    }


def reference(x, norm1_g, w_in, conv_w, conv_b, conv_g, ret_g, q_g, k_g,
              w_out, norm2_g, w_ff1, w_ff2, rel_bias):
    for l in range(DEPTH):
        x = hybrid_layer(x, norm1_g[l], w_in[l], conv_w[l], conv_b[l], conv_g[l], ret_g[l],
                         q_g[l], k_g[l], w_out[l], norm2_g[l], w_ff1[l], w_ff2[l], rel_bias)
    return x
```

```python
import functools
import math

import jax
import jax.numpy as jnp
from jax import lax
from jax.experimental import pallas as pl
from jax.experimental.pallas import tpu as pltpu

F32 = jnp.float32
BF16 = jnp.bfloat16

D_MODEL = 1024
CONV_CH = 256
CONV_GROUP = 64
CONV_WIDTH = 31
RET_HEADS = 6
RET_KDIM = 32
RET_VDIM = 64
RET_W = RET_HEADS * RET_VDIM
ATT_HEADS = 6
ATT_HDIM = 64
ATT_W = ATT_HEADS * ATT_HDIM
DILATIONS = (1, 4, 16)
ATT_BLOCK = 128
N_BUCKETS = 32
MAX_DISTANCE = 2048
D_FF = 4 * D_MODEL
EPS = 1e-6
ROPE_BASE = 10000.0
NEG_INF = -1e30

LANES = 128
RQK_PAD = 256
VMEM_LIMIT_BYTES = 56 * 1024 * 1024

_IN_SIZES = (2 * CONV_CH, RET_HEADS * RET_KDIM, RET_HEADS * RET_KDIM, RET_W, RET_W, ATT_W, ATT_W, ATT_W)
C_CONV = 0
C_RQ = 512
C_RK = C_RQ + RQK_PAD
C_RV = C_RK + RQK_PAD
C_RG = C_RV + RET_W
C_AQ = C_RG + RET_W
C_AK = C_AQ + ATT_W
C_AV = C_AK + ATT_W
IN_W_PAD = C_AV + ATT_W

TM_PROJ = 512
TM_FFN = 512
TS_CONV = 1024
CONV_ROWS = 64
CONV_HALO = 32
TR_RET = 512
RET_CHUNK = 128
SPAN = ATT_BLOCK * DILATIONS[-1]
FF_CHUNK = 1024


def _const_spec(shape):
    nd = len(shape)
    return pl.BlockSpec(shape, lambda *_: (0,) * nd, pipeline_mode=pl.Buffered(1))


def _params(*sem):
    return pltpu.CompilerParams(dimension_semantics=sem, vmem_limit_bytes=VMEM_LIMIT_BYTES)


def _segment_mean(y, ones_ref, width):
    hi = y.astype(BF16)
    lo = (y - hi.astype(F32)).astype(BF16)
    ones = ones_ref[...]
    tot = jnp.dot(hi, ones, preferred_element_type=F32) + jnp.dot(lo, ones, preferred_element_type=F32)
    return tot * (1.0 / width)


def _in_proj_kernel(x_ref, g1_ref, w_ref, cos_ref, sin_ref, qg_ref, kg_ref, ones_ref,
                    hc_ref, rq_ref, rk_ref, rv_ref, rg_ref, aq_ref, ak_ref, av_ref):
    x = x_ref[...]
    h = x * lax.rsqrt(jnp.mean(x * x, axis=-1, keepdims=True) + EPS) * g1_ref[...]
    hb = h.astype(BF16)

    def proj(lo, hi):
        return jnp.dot(hb, w_ref[:, lo:hi], preferred_element_type=F32)

    u = proj(C_CONV, C_RQ)
    hc_ref[...] = u[:, :CONV_CH] * jax.nn.sigmoid(u[:, CONV_CH:])

    qk = proj(C_RQ, C_RV)
    even = (lax.broadcasted_iota(jnp.int32, (1, LANES), 1) % 2) == 0
    cos = cos_ref[...]
    sin = sin_ref[...]
    for half, dst in ((0, rq_ref), (1, rk_ref)):
        for c in range(RQK_PAD // LANES):
            t = qk[:, half * RQK_PAD + c * LANES: half * RQK_PAD + (c + 1) * LANES]
            sw = jnp.where(even, pltpu.roll(t, LANES - 1, 1), pltpu.roll(t, 1, 1))
            sl = slice(c * LANES, (c + 1) * LANES)
            dst[:, sl] = (t * cos[:, sl] + sw * sin[:, sl]).astype(BF16)

    rv_ref[...] = proj(C_RV, C_RG).astype(BF16)
    g = proj(C_RG, C_AQ)
    rg_ref[...] = (g * jax.nn.sigmoid(g)).astype(BF16)

    aq = proj(C_AQ, C_AK)
    aq = aq * lax.rsqrt(_segment_mean(aq * aq, ones_ref, ATT_HDIM) + EPS) * qg_ref[...]
    ak = proj(C_AK, C_AV)
    ak = ak * lax.rsqrt(_segment_mean(ak * ak, ones_ref, ATT_HDIM) + EPS) * kg_ref[...]
    av = proj(C_AV, IN_W_PAD)
    for p in range(ATT_W // LANES):
        sl = slice(p * LANES, (p + 1) * LANES)
        aq_ref[p] = aq[:, sl]
        ak_ref[p] = ak[:, sl]
        av_ref[p] = av[:, sl]


def _in_proj(x2, g1, w_in, cos_t, sin_t, qg, kg, ones384, seq):
    T = x2.shape[0]
    tm = TM_PROJ
    nseq = seq // tm
    row = lambda i: (i, 0)
    pos = lambda i: (i % nseq, 0)
    slab = lambda i: (0, i, 0)
    return pl.pallas_call(
        _in_proj_kernel,
        grid=(T // tm,),
        in_specs=[
            pl.BlockSpec((tm, D_MODEL), row),
            _const_spec((1, D_MODEL)),
            _const_spec((D_MODEL, IN_W_PAD)),
            pl.BlockSpec((tm, RQK_PAD), pos),
            pl.BlockSpec((tm, RQK_PAD), pos),
            _const_spec((1, ATT_W)),
            _const_spec((1, ATT_W)),
            _const_spec((ATT_W, ATT_W)),
        ],
        out_specs=[
            pl.BlockSpec((tm, CONV_CH), row),
            pl.BlockSpec((tm, RQK_PAD), row),
            pl.BlockSpec((tm, RQK_PAD), row),
            pl.BlockSpec((tm, RET_W), row),
            pl.BlockSpec((tm, RET_W), row),
            pl.BlockSpec((ATT_W // LANES, tm, LANES), slab),
            pl.BlockSpec((ATT_W // LANES, tm, LANES), slab),
            pl.BlockSpec((ATT_W // LANES, tm, LANES), slab),
        ],
        out_shape=[
            jax.ShapeDtypeStruct((T, CONV_CH), F32),
            jax.ShapeDtypeStruct((T, RQK_PAD), BF16),
            jax.ShapeDtypeStruct((T, RQK_PAD), BF16),
            jax.ShapeDtypeStruct((T, RET_W), BF16),
            jax.ShapeDtypeStruct((T, RET_W), BF16),
            jax.ShapeDtypeStruct((ATT_W // LANES, T, LANES), F32),
            jax.ShapeDtypeStruct((ATT_W // LANES, T, LANES), F32),
            jax.ShapeDtypeStruct((ATT_W // LANES, T, LANES), F32),
        ],
        compiler_params=_params("arbitrary"),
        name="in_proj",
    )(x2, g1, w_in, cos_t, sin_t, qg, kg, ones384)


def _conv_kernel(hc_ref, cw_ref, cb_ref, cg_ref, ones_ref, out_ref, ext_ref):
    ts = hc_ref.shape[0]
    nslab = CONV_CH // LANES

    @pl.when(pl.program_id(1) == 0)
    def _():
        ext_ref[:, 0:CONV_HALO, :] = jnp.zeros((nslab, CONV_HALO, LANES), F32)

    for j in range(nslab):
        ext_ref[j, CONV_HALO:CONV_HALO + ts, :] = hc_ref[:, j * LANES:(j + 1) * LANES]

    first_tap = CONV_HALO - (CONV_WIDTH - 1)

    def chunk(c, carry):
        r0 = pl.multiple_of(c * CONV_ROWS, CONV_ROWS)
        parts = []
        for j in range(nslab):
            sl = slice(j * LANES, (j + 1) * LANES)
            acc = jnp.broadcast_to(cb_ref[:, sl], (CONV_ROWS, LANES))
            for w in range(CONV_WIDTH):
                acc = acc + ext_ref[j, pl.ds(r0 + first_tap + w, CONV_ROWS), :] * cw_ref[w:w + 1, sl]
            parts.append(acc)
        hcv = jnp.concatenate(parts, axis=1)
        ms = _segment_mean(hcv * hcv, ones_ref, CONV_GROUP)
        y = hcv * lax.rsqrt(ms + EPS) * cg_ref[...]
        out_ref[pl.ds(r0, CONV_ROWS), :] = (y * jax.nn.sigmoid(y)).astype(BF16)
        return carry

    lax.fori_loop(0, ts // CONV_ROWS, chunk, 0)
    ext_ref[:, 0:CONV_HALO, :] = ext_ref[:, ts:ts + CONV_HALO, :]


def _conv(hc, cw, cb, cg, ones256, batch, seq):
    ts = TS_CONV
    nseq = seq // ts
    row = lambda b, i: (b * nseq + i, 0)
    return pl.pallas_call(
        _conv_kernel,
        grid=(batch, nseq),
        in_specs=[
            pl.BlockSpec((ts, CONV_CH), row),
            _const_spec((CONV_WIDTH, CONV_CH)),
            _const_spec((1, CONV_CH)),
            _const_spec((1, CONV_CH)),
            _const_spec((CONV_CH, CONV_CH)),
        ],
        out_specs=pl.BlockSpec((ts, CONV_CH), row),
        out_shape=jax.ShapeDtypeStruct((batch * seq, CONV_CH), BF16),
        scratch_shapes=[pltpu.VMEM((CONV_CH // LANES, CONV_HALO + ts, LANES), F32)],
        compiler_params=_params("arbitrary", "arbitrary"),
        name="conv",
    )(hc, cw, cb, cg, ones256)


def _ret_kernel(rq_ref, rk_ref, rv_ref, rg_ref, decay_ref, qdec_ref, kdec_ref, sdec_ref, bd_ref,
                retg_ref, ones_ref, out_ref, state_ref):
    @pl.when(pl.program_id(1) == 0)
    def _():
        state_ref[...] = jnp.zeros(state_ref.shape, F32)

    C = RET_CHUNK
    qlane_head = lax.broadcasted_iota(jnp.int32, (C, RQK_PAD), 1) // RET_KDIM
    vlane_first = (lax.broadcasted_iota(jnp.int32, (C, LANES), 1) // RET_VDIM) == 0
    npair = RET_W // LANES

    for c in range(rq_ref.shape[0] // C):
        rows = slice(c * C, (c + 1) * C)
        q = rq_ref[rows, :]
        k = rk_ref[rows, :]
        v = rv_ref[rows, :]

        zq = jnp.zeros_like(q)
        qs = jnp.concatenate([jnp.where(qlane_head == h, q, zq) for h in range(RET_HEADS)], axis=0)
        sc = lax.dot_general(qs, k, (((1,), (1,)), ((), ())), preferred_element_type=F32)
        p = (sc * decay_ref[...]).astype(BF16)

        inner = []
        for pr in range(npair):
            vp = v[:, pr * LANES:(pr + 1) * LANES]
            zv = jnp.zeros_like(vp)
            pcat = jnp.concatenate([p[(2 * pr) * C:(2 * pr + 1) * C], p[(2 * pr + 1) * C:(2 * pr + 2) * C]], axis=1)
            vst = jnp.concatenate([jnp.where(vlane_first, vp, zv), jnp.where(vlane_first, zv, vp)], axis=0)
            inner.append(jnp.dot(pcat, vst, preferred_element_type=F32))
        inner = jnp.concatenate(inner, axis=1)

        state = state_ref[...]
        cross = jnp.dot(q, state.astype(BF16), preferred_element_type=F32) * qdec_ref[...]
        vdec = (v.astype(F32) * kdec_ref[...]).astype(BF16)
        kv = lax.dot_general(k, vdec, (((0,), (0,)), ((), ())), preferred_element_type=F32)
        state_ref[...] = state * sdec_ref[...] + kv * bd_ref[...]

        o = inner + cross
        o = o * lax.rsqrt(_segment_mean(o * o, ones_ref, RET_VDIM) + EPS) * retg_ref[...]
        out_ref[rows, :] = (o * rg_ref[rows, :].astype(F32)).astype(BF16)


def _retention(rq, rk, rv, rg, tables, retg, ones384, batch, seq):
    tr = TR_RET
    nseq = seq // tr
    row = lambda b, i: (b * nseq + i, 0)
    decay, qdec, kdec, sdec, bd = tables
    return pl.pallas_call(
        _ret_kernel,
        grid=(batch, nseq),
        in_specs=[
            pl.BlockSpec((tr, RQK_PAD), row),
            pl.BlockSpec((tr, RQK_PAD), row),
            pl.BlockSpec((tr, RET_W), row),
            pl.BlockSpec((tr, RET_W), row),
            _const_spec(decay.shape),
            _const_spec(qdec.shape),
            _const_spec(kdec.shape),
            _const_spec(sdec.shape),
            _const_spec(bd.shape),
            _const_spec((1, RET_W)),
            _const_spec((RET_W, RET_W)),
        ],
        out_specs=pl.BlockSpec((tr, RET_W), row),
        out_shape=jax.ShapeDtypeStruct((batch * seq, RET_W), BF16),
        scratch_shapes=[pltpu.VMEM((RQK_PAD, RET_W), F32)],
        compiler_params=_params("arbitrary", "arbitrary"),
        name="retention",
    )(rq, rk, rv, rg, decay, qdec, kdec, sdec, bd, retg, ones384)


def _attn_kernel(q_ref, k_ref, v_ref, tbl_ref, out_ref,
                 kb1, vb1, kb4, vb4, kb16, vb16,
                 acc1, m1, s1, acc4, m4, s4, acc16, m16, s16):
    span_i = pl.program_id(2)
    L = ATT_BLOCK
    branches = ((1, kb1, vb1, acc1, m1, s1), (4, kb4, vb4, acc4, m4, s4), (16, kb16, vb16, acc16, m16, s16))

    @pl.when(span_i == 0)
    def _():
        for d, kb, vb, _, _, _ in branches:
            kb[0:L * d, :] = jnp.zeros((L * d, LANES), BF16)
            vb[0:L * d, :] = jnp.zeros((L * d, LANES), BF16)

    first_head = lax.broadcasted_iota(jnp.int32, (L, LANES), 1) < ATT_HDIM
    first_head2 = lax.broadcasted_iota(jnp.int32, (2 * L, LANES), 1) < ATT_HDIM

    for b, (d, kb, vb, acc_r, m_r, s_r) in enumerate(branches):
        prevlen = L * d
        shift = int(math.log2(d))

        def unit(u, carry, b=b, d=d, kb=kb, vb=vb, acc_r=acc_r, m_r=m_r, s_r=s_r, prevlen=prevlen, shift=shift):
            sub = lax.shift_right_logical(u, shift)
            res = u - lax.shift_left(sub, shift)
            off = pl.multiple_of(u * L, L)
            if d == 1:
                rows = pl.ds(off, L)
            else:
                rows = pl.ds(sub * (L * d) + res, L, stride=d)
            q = q_ref[rows, :].astype(BF16)
            kc = k_ref[rows, :].astype(BF16)
            vc = v_ref[rows, :].astype(BF16)
            kp = kb[pl.ds(off, L), :]
            vp = vb[pl.ds(off, L), :]
            kb[pl.ds(off + prevlen, L), :] = kc
            vb[pl.ds(off + prevlen, L), :] = vc
            kcat = jnp.concatenate([kp, kc], axis=0)
            vcat = jnp.concatenate([vp, vc], axis=0)
            zq = jnp.zeros_like(q)
            q2 = jnp.concatenate([jnp.where(first_head, q, zq), jnp.where(first_head, zq, q)], axis=0)
            variant = jnp.logical_and(span_i == 0, sub == 0).astype(jnp.int32)
            logits = lax.dot_general(q2, kcat, (((1,), (1,)), ((), ())), preferred_element_type=F32)
            logits = logits + tbl_ref[b, variant]
            m = jnp.max(logits, axis=-1, keepdims=True)
            p = jnp.exp(logits - m)
            s = jnp.sum(p, axis=-1, keepdims=True)
            pb = p.astype(BF16)
            pcat = jnp.concatenate([pb[:L], pb[L:]], axis=1)
            zv = jnp.zeros_like(vcat)
            vst = jnp.concatenate([jnp.where(first_head2, vcat, zv), jnp.where(first_head2, zv, vcat)], axis=0)
            acc = jnp.dot(pcat, vst, preferred_element_type=F32)
            acc_r[rows, :] = acc
            m_r[rows, :] = jnp.where(first_head, m[:L], m[L:])
            s_r[rows, :] = jnp.where(first_head, s[:L], s[L:])
            return carry

        lax.fori_loop(0, SPAN // L, unit, 0)

    CR = 256

    def combine(c, carry):
        rows = pl.ds(pl.multiple_of(c * CR, CR), CR)
        ma, mb, mc = m1[rows, :], m4[rows, :], m16[rows, :]
        mx = jnp.maximum(jnp.maximum(ma, mb), mc)
        ea, eb, ec = jnp.exp(ma - mx), jnp.exp(mb - mx), jnp.exp(mc - mx)
        num = ea * acc1[rows, :] + eb * acc4[rows, :] + ec * acc16[rows, :]
        den = ea * s1[rows, :] + eb * s4[rows, :] + ec * s16[rows, :]
        out_ref[rows, :] = (num / den).astype(BF16)
        return carry

    lax.fori_loop(0, SPAN // CR, combine, 0)

    for d, kb, vb, _, _, _ in branches:
        kb[0:L * d, :] = kb[SPAN:SPAN + L * d, :]
        vb[0:L * d, :] = vb[SPAN:SPAN + L * d, :]


def _attention(aq, ak, av, tbl, batch, seq):
    nslab = ATT_W // LANES
    nspan = seq // SPAN
    blk = lambda b, p, i: (p, b * nspan + i, 0)
    scratch = []
    for d in DILATIONS:
        scratch += [pltpu.VMEM((ATT_BLOCK * d + SPAN, LANES), BF16)] * 2
    scratch += [pltpu.VMEM((SPAN, LANES), F32)] * 9
    return pl.pallas_call(
        _attn_kernel,
        grid=(batch, nslab, nspan),
        in_specs=[
            pl.BlockSpec((None, SPAN, LANES), blk),
            pl.BlockSpec((None, SPAN, LANES), blk),
            pl.BlockSpec((None, SPAN, LANES), blk),
            pl.BlockSpec((None, len(DILATIONS), 2, 2 * ATT_BLOCK, 2 * ATT_BLOCK), lambda b, p, i: (p, 0, 0, 0, 0)),
        ],
        out_specs=pl.BlockSpec((None, SPAN, LANES), blk),
        out_shape=jax.ShapeDtypeStruct((nslab, batch * seq, LANES), BF16),
        scratch_shapes=scratch,
        compiler_params=_params("arbitrary", "arbitrary", "arbitrary"),
        name="dilated_attn",
    )(aq, ak, av, tbl)


def _out_ffn_kernel(x_ref, conv_ref, ret_ref, att_ref, wout_ref, g2_ref, w1_ref, w2_ref, o_ref):
    mix = jnp.concatenate([conv_ref[...], ret_ref[...]] + [att_ref[p] for p in range(ATT_W // LANES)], axis=1)
    x1 = x_ref[...] + jnp.dot(mix, wout_ref[...], preferred_element_type=F32)
    h2 = (x1 * lax.rsqrt(jnp.mean(x1 * x1, axis=-1, keepdims=True) + EPS) * g2_ref[...]).astype(BF16)
    acc = x1
    for c in range(D_FF // FF_CHUNK):
        sl = slice(c * FF_CHUNK, (c + 1) * FF_CHUNK)
        t = jnp.maximum(jnp.dot(h2, w1_ref[:, sl], preferred_element_type=F32), 0.0)
        acc = acc + jnp.dot((t * t).astype(BF16), w2_ref[sl, :], preferred_element_type=F32)
    o_ref[...] = acc


def _out_ffn(x2, conv_o, ret_o, att_o, w_out, g2, w1, w2):
    T = x2.shape[0]
    tm = TM_FFN
    row = lambda i: (i, 0)
    return pl.pallas_call(
        _out_ffn_kernel,
        grid=(T // tm,),
        in_specs=[
            pl.BlockSpec((tm, D_MODEL), row),
            pl.BlockSpec((tm, CONV_CH), row),
            pl.BlockSpec((tm, RET_W), row),
            pl.BlockSpec((ATT_W // LANES, tm, LANES), lambda i: (0, i, 0)),
            _const_spec((D_MODEL, D_MODEL)),
            _const_spec((1, D_MODEL)),
            _const_spec((D_MODEL, D_FF)),
            _const_spec((D_FF, D_MODEL)),
        ],
        out_specs=pl.BlockSpec((tm, D_MODEL), row),
        out_shape=jax.ShapeDtypeStruct((T, D_MODEL), F32),
        compiler_params=_params("arbitrary"),
        name="out_ffn",
    )(x2, conv_o, ret_o, att_o, w_out, g2, w1, w2)


def _rotary_tables(seq):
    half = RET_KDIM // 2
    inv = 1.0 / (ROPE_BASE ** jnp.linspace(0.0, 1.0, half, dtype=F32))
    ang = jnp.arange(seq, dtype=F32)[:, None] * inv[None, :]
    c, s = jnp.cos(ang), jnp.sin(ang)
    cos_h = jnp.repeat(c, 2, axis=1)
    sin_h = jnp.stack([-s, s], axis=-1).reshape(seq, RET_KDIM)
    pad = jnp.zeros((seq, RQK_PAD - RET_HEADS * RET_KDIM), F32)
    cos_t = jnp.concatenate([jnp.tile(cos_h, (1, RET_HEADS)), pad], axis=1)
    sin_t = jnp.concatenate([jnp.tile(sin_h, (1, RET_HEADS)), pad], axis=1)
    return cos_t, sin_t


def _retention_tables():
    C = RET_CHUNK
    H = RET_HEADS
    scale = RET_KDIM ** -0.5
    log_gamma = jnp.log(1.0 - 2.0 ** (-5.0 - jnp.arange(H, dtype=F32)))
    idx = jnp.arange(C, dtype=F32)
    diff = idx[:, None] - idx[None, :]
    decay = jnp.where(diff >= 0, jnp.exp(log_gamma[:, None, None] * jnp.maximum(diff, 0.0)), 0.0)
    decay = (decay * scale).reshape(H * C, C)
    lane_head = jnp.arange(RET_W) // RET_VDIM
    qdec = jnp.exp(log_gamma[lane_head][None, :] * (idx[:, None] + 1.0))
    kdec = jnp.exp(log_gamma[lane_head][None, :] * (C - 1.0 - idx[:, None])) * scale
    row_head = jnp.arange(RQK_PAD) // RET_KDIM
    bd = (row_head[:, None] == lane_head[None, :]).astype(F32)
    sdec = bd * jnp.exp(log_gamma * C)[lane_head][None, :]
    return decay, qdec, kdec, sdec, bd


def _t5_bucket(dist):
    max_exact = N_BUCKETS // 2
    nf = jnp.maximum(dist, 1).astype(F32)
    large = max_exact + (jnp.log(nf / max_exact) / math.log(MAX_DISTANCE / max_exact)
                         * (N_BUCKETS - max_exact)).astype(jnp.int32)
    large = jnp.minimum(large, N_BUCKETS - 1)
    return jnp.where(dist < max_exact, dist, large)


def _attention_tables(rel_bias):
    Lb = ATT_BLOCK
    qi = jnp.arange(Lb, dtype=jnp.int32)[:, None]
    kj = jnp.arange(2 * Lb, dtype=jnp.int32)[None, :]
    dist = Lb + qi - kj
    band = (dist >= 0) & (dist <= Lb)
    per_branch = []
    for d in DILATIONS:
        bias = rel_bias.astype(F32)[_t5_bucket(jnp.clip(dist, 0, Lb) * d)]
        bias = bias.transpose(2, 0, 1)
        full = jnp.where(band[None], bias, NEG_INF)
        first = jnp.where((band & (kj >= Lb))[None], bias, NEG_INF)
        per_branch.append(jnp.stack([full, first], axis=0))
    t = jnp.stack(per_branch, axis=0)
    nslab = ATT_W // LANES
    t = t.reshape(len(DILATIONS), 2, nslab, 2 * Lb, 2 * Lb)
    return t.transpose(2, 0, 1, 3, 4)


def _pad_w_in(w_in):
    splits = [sum(_IN_SIZES[:i + 1]) for i in range(len(_IN_SIZES) - 1)]
    parts = jnp.split(w_in, splits, axis=1)
    zpad = jnp.zeros((w_in.shape[0], RQK_PAD - RET_HEADS * RET_KDIM), w_in.dtype)
    return jnp.concatenate([parts[0], parts[1], zpad, parts[2], zpad] + parts[3:], axis=1)


def _block_ones(n, width):
    g = jnp.arange(n) // width
    return (g[:, None] == g[None, :]).astype(BF16)


def kernel(x, norm1_g, w_in, conv_w, conv_b, conv_g, ret_g, q_g, k_g, w_out, norm2_g, w_ff1, w_ff2, rel_bias):
    batch, seq, _ = x.shape
    depth = w_in.shape[0]
    assert seq % SPAN == 0 and seq % TS_CONV == 0 and seq % TR_RET == 0 and seq % TM_PROJ == 0
    assert (batch * seq) % TM_FFN == 0

    cos_t, sin_t = _rotary_tables(seq)
    ret_tables = _retention_tables()
    att_tbl = _attention_tables(rel_bias)
    ones384 = _block_ones(ATT_W, ATT_HDIM)
    ones256 = _block_ones(CONV_CH, CONV_GROUP)

    x2 = x.reshape(batch * seq, D_MODEL)
    for l in range(depth):
        w_in_l = _pad_w_in(w_in[l]).astype(BF16)
        qg = jnp.tile(q_g[l].astype(F32), ATT_HEADS)[None, :] * (ATT_HDIM ** -0.5)
        kg = jnp.tile(k_g[l].astype(F32), ATT_HEADS)[None, :]
        hc, rq, rk, rv, rg, aq, ak, av = _in_proj(
            x2, norm1_g[l].astype(F32)[None, :], w_in_l, cos_t, sin_t, qg, kg, ones384, seq)
        conv_o = _conv(hc, conv_w[l].astype(F32), conv_b[l].astype(F32)[None, :],
                       conv_g[l].astype(F32)[None, :], ones256, batch, seq)
        ret_o = _retention(rq, rk, rv, rg, ret_tables, ret_g[l].astype(F32)[None, :], ones384, batch, seq)
        att_o = _attention(aq, ak, av, att_tbl, batch, seq)
        x2 = _out_ffn(x2, conv_o, ret_o, att_o, w_out[l].astype(BF16), norm2_g[l].astype(F32)[None, :],
                      w_ff1[l].astype(BF16), w_ff2[l].astype(BF16))
    return x2.reshape(batch, seq, D_MODEL)
```

```python
import functools
import math

import jax
import jax.numpy as jnp
from jax import lax
from jax.experimental import pallas as pl
from jax.experimental.pallas import tpu as pltpu

F32 = jnp.float32
BF16 = jnp.bfloat16

D_MODEL = 1024
CONV_CH = 256
CONV_GROUP = 64
CONV_WIDTH = 31
RET_HEADS = 6
RET_KDIM = 32
RET_VDIM = 64
RET_W = RET_HEADS * RET_VDIM
ATT_HEADS = 6
ATT_HDIM = 64
ATT_W = ATT_HEADS * ATT_HDIM
DILATIONS = (1, 4, 16)
ATT_BLOCK = 128
N_BUCKETS = 32
MAX_DISTANCE = 2048
D_FF = 4 * D_MODEL
EPS = 1e-6
ROPE_BASE = 10000.0
NEG_INF = -1e30
LOG2_E = math.log2(math.e)

LANES = 128
RQK_PAD = 256
VMEM_LIMIT_BYTES = 56 * 1024 * 1024

_IN_SIZES = (2 * CONV_CH, RET_HEADS * RET_KDIM, RET_HEADS * RET_KDIM, RET_W, RET_W, ATT_W, ATT_W, ATT_W)
C_CONV = 0
C_RQ = 512
C_RK = C_RQ + RQK_PAD
C_RV = C_RK + RQK_PAD
C_RG = C_RV + RET_W
C_AQ = C_RG + RET_W
C_AK = C_AQ + ATT_W
C_AV = C_AK + ATT_W
IN_W_PAD = C_AV + ATT_W

TM_PROJ = 512
TM_FFN = 512
TS_CONV = 1024
CONV_ROWS = 64
CONV_HALO = 32
TR_RET = 512
RET_CHUNK = 128
SPAN = ATT_BLOCK * DILATIONS[-1]
FF_CHUNK = 1024
UNIT_UNROLL = 8


def _const_spec(shape):
    nd = len(shape)
    return pl.BlockSpec(shape, lambda *_: (0,) * nd, pipeline_mode=pl.Buffered(1))


def _params(*sem):
    return pltpu.CompilerParams(dimension_semantics=sem, vmem_limit_bytes=VMEM_LIMIT_BYTES)


def _segment_mean(y, ones_ref, width):
    hi = y.astype(BF16)
    lo = (y - hi.astype(F32)).astype(BF16)
    ones = ones_ref[...]
    tot = jnp.dot(hi, ones, preferred_element_type=F32) + jnp.dot(lo, ones, preferred_element_type=F32)
    return tot * (1.0 / width)


def _in_proj_kernel(x_ref, g1_ref, w_ref, cos_ref, sin_ref, qg_ref, kg_ref, ones_ref,
                    hc_ref, rq_ref, rk_ref, rv_ref, rg_ref, aq_ref, ak_ref, av_ref):
    x = x_ref[...]
    h = x * lax.rsqrt(jnp.mean(x * x, axis=-1, keepdims=True) + EPS) * g1_ref[...]
    hb = h.astype(BF16)

    def proj(lo, hi):
        return jnp.dot(hb, w_ref[:, lo:hi], preferred_element_type=F32)

    u = proj(C_CONV, C_RQ)
    hc_ref[...] = u[:, :CONV_CH] * jax.nn.sigmoid(u[:, CONV_CH:])

    qk = proj(C_RQ, C_RV)
    even = (lax.broadcasted_iota(jnp.int32, (1, LANES), 1) % 2) == 0
    cos = cos_ref[...]
    sin = sin_ref[...]
    for half, dst in ((0, rq_ref), (1, rk_ref)):
        for c in range(RQK_PAD // LANES):
            t = qk[:, half * RQK_PAD + c * LANES: half * RQK_PAD + (c + 1) * LANES]
            sw = jnp.where(even, pltpu.roll(t, LANES - 1, 1), pltpu.roll(t, 1, 1))
            sl = slice(c * LANES, (c + 1) * LANES)
            dst[:, sl] = (t * cos[:, sl] + sw * sin[:, sl]).astype(BF16)

    rv_ref[...] = proj(C_RV, C_RG).astype(BF16)
    g = proj(C_RG, C_AQ)
    rg_ref[...] = (g * jax.nn.sigmoid(g)).astype(BF16)

    aq = proj(C_AQ, C_AK)
    aq = aq * lax.rsqrt(_segment_mean(aq * aq, ones_ref, ATT_HDIM) + EPS) * qg_ref[...]
    ak = proj(C_AK, C_AV)
    ak = ak * lax.rsqrt(_segment_mean(ak * ak, ones_ref, ATT_HDIM) + EPS) * kg_ref[...]
    av = proj(C_AV, IN_W_PAD)
    for p in range(ATT_W // LANES):
        sl = slice(p * LANES, (p + 1) * LANES)
        aq_ref[p] = aq[:, sl]
        ak_ref[p] = ak[:, sl]
        av_ref[p] = av[:, sl]


def _in_proj(x2, g1, w_in, cos_t, sin_t, qg, kg, ones384, seq):
    T = x2.shape[0]
    tm = TM_PROJ
    nseq = seq // tm
    row = lambda i: (i, 0)
    pos = lambda i: (i % nseq, 0)
    slab = lambda i: (0, i, 0)
    return pl.pallas_call(
        _in_proj_kernel,
        grid=(T // tm,),
        in_specs=[
            pl.BlockSpec((tm, D_MODEL), row),
            _const_spec((1, D_MODEL)),
            _const_spec((D_MODEL, IN_W_PAD)),
            pl.BlockSpec((tm, RQK_PAD), pos),
            pl.BlockSpec((tm, RQK_PAD), pos),
            _const_spec((1, ATT_W)),
            _const_spec((1, ATT_W)),
            _const_spec((ATT_W, ATT_W)),
        ],
        out_specs=[
            pl.BlockSpec((tm, CONV_CH), row),
            pl.BlockSpec((tm, RQK_PAD), row),
            pl.BlockSpec((tm, RQK_PAD), row),
            pl.BlockSpec((tm, RET_W), row),
            pl.BlockSpec((tm, RET_W), row),
            pl.BlockSpec((ATT_W // LANES, tm, LANES), slab),
            pl.BlockSpec((ATT_W // LANES, tm, LANES), slab),
            pl.BlockSpec((ATT_W // LANES, tm, LANES), slab),
        ],
        out_shape=[
            jax.ShapeDtypeStruct((T, CONV_CH), F32),
            jax.ShapeDtypeStruct((T, RQK_PAD), BF16),
            jax.ShapeDtypeStruct((T, RQK_PAD), BF16),
            jax.ShapeDtypeStruct((T, RET_W), BF16),
            jax.ShapeDtypeStruct((T, RET_W), BF16),
            jax.ShapeDtypeStruct((ATT_W // LANES, T, LANES), F32),
            jax.ShapeDtypeStruct((ATT_W // LANES, T, LANES), F32),
            jax.ShapeDtypeStruct((ATT_W // LANES, T, LANES), F32),
        ],
        compiler_params=_params("arbitrary"),
        name="in_proj",
    )(x2, g1, w_in, cos_t, sin_t, qg, kg, ones384)


def _conv_kernel(hc_ref, cw_ref, cb_ref, cg_ref, ones_ref, out_ref, ext_ref):
    ts = hc_ref.shape[0]
    nslab = CONV_CH // LANES

    @pl.when(pl.program_id(1) == 0)
    def _():
        ext_ref[:, 0:CONV_HALO, :] = jnp.zeros((nslab, CONV_HALO, LANES), F32)

    for j in range(nslab):
        ext_ref[j, CONV_HALO:CONV_HALO + ts, :] = hc_ref[:, j * LANES:(j + 1) * LANES]

    first_tap = CONV_HALO - (CONV_WIDTH - 1)

    def chunk(c, carry):
        r0 = pl.multiple_of(c * CONV_ROWS, CONV_ROWS)
        parts = []
        for j in range(nslab):
            sl = slice(j * LANES, (j + 1) * LANES)
            acc = jnp.broadcast_to(cb_ref[:, sl], (CONV_ROWS, LANES))
            for w in range(CONV_WIDTH):
                acc = acc + ext_ref[j, pl.ds(r0 + first_tap + w, CONV_ROWS), :] * cw_ref[w:w + 1, sl]
            parts.append(acc)
        hcv = jnp.concatenate(parts, axis=1)
        ms = _segment_mean(hcv * hcv, ones_ref, CONV_GROUP)
        y = hcv * lax.rsqrt(ms + EPS) * cg_ref[...]
        out_ref[pl.ds(r0, CONV_ROWS), :] = (y * jax.nn.sigmoid(y)).astype(BF16)
        return carry

    lax.fori_loop(0, ts // CONV_ROWS, chunk, 0)
    ext_ref[:, 0:CONV_HALO, :] = ext_ref[:, ts:ts + CONV_HALO, :]


def _conv(hc, cw, cb, cg, ones256, batch, seq):
    ts = TS_CONV
    nseq = seq // ts
    row = lambda b, i: (b * nseq + i, 0)
    return pl.pallas_call(
        _conv_kernel,
        grid=(batch, nseq),
        in_specs=[
            pl.BlockSpec((ts, CONV_CH), row),
            _const_spec((CONV_WIDTH, CONV_CH)),
            _const_spec((1, CONV_CH)),
            _const_spec((1, CONV_CH)),
            _const_spec((CONV_CH, CONV_CH)),
        ],
        out_specs=pl.BlockSpec((ts, CONV_CH), row),
        out_shape=jax.ShapeDtypeStruct((batch * seq, CONV_CH), BF16),
        scratch_shapes=[pltpu.VMEM((CONV_CH // LANES, CONV_HALO + ts, LANES), F32)],
        compiler_params=_params("arbitrary", "arbitrary"),
        name="conv",
    )(hc, cw, cb, cg, ones256)


def _ret_kernel(rq_ref, rk_ref, rv_ref, rg_ref, decay_ref, qdec_ref, kdec_ref, sdec_ref, bd_ref,
                retg_ref, ones_ref, out_ref, state_ref):
    @pl.when(pl.program_id(1) == 0)
    def _():
        state_ref[...] = jnp.zeros(state_ref.shape, F32)

    C = RET_CHUNK
    qlane_head = lax.broadcasted_iota(jnp.int32, (C, RQK_PAD), 1) // RET_KDIM
    vlane_first = (lax.broadcasted_iota(jnp.int32, (C, LANES), 1) // RET_VDIM) == 0
    npair = RET_W // LANES

    for c in range(rq_ref.shape[0] // C):
        rows = slice(c * C, (c + 1) * C)
        q = rq_ref[rows, :]
        k = rk_ref[rows, :]
        v = rv_ref[rows, :]

        zq = jnp.zeros_like(q)
        qs = jnp.concatenate([jnp.where(qlane_head == h, q, zq) for h in range(RET_HEADS)], axis=0)
        sc = lax.dot_general(qs, k, (((1,), (1,)), ((), ())), preferred_element_type=F32)
        p = (sc * decay_ref[...]).astype(BF16)

        inner = []
        for pr in range(npair):
            vp = v[:, pr * LANES:(pr + 1) * LANES]
            zv = jnp.zeros_like(vp)
            pcat = jnp.concatenate([p[(2 * pr) * C:(2 * pr + 1) * C], p[(2 * pr + 1) * C:(2 * pr + 2) * C]], axis=1)
            vst = jnp.concatenate([jnp.where(vlane_first, vp, zv), jnp.where(vlane_first, zv, vp)], axis=0)
            inner.append(jnp.dot(pcat, vst, preferred_element_type=F32))
        inner = jnp.concatenate(inner, axis=1)

        state = state_ref[...]
        cross = jnp.dot(q, state.astype(BF16), preferred_element_type=F32) * qdec_ref[...]
        vdec = (v.astype(F32) * kdec_ref[...]).astype(BF16)
        kv = lax.dot_general(k, vdec, (((0,), (0,)), ((), ())), preferred_element_type=F32)
        state_ref[...] = state * sdec_ref[...] + kv * bd_ref[...]

        o = inner + cross
        o = o * lax.rsqrt(_segment_mean(o * o, ones_ref, RET_VDIM) + EPS) * retg_ref[...]
        out_ref[rows, :] = (o * rg_ref[rows, :].astype(F32)).astype(BF16)


def _retention(rq, rk, rv, rg, tables, retg, ones384, batch, seq):
    tr = TR_RET
    nseq = seq // tr
    row = lambda b, i: (b * nseq + i, 0)
    decay, qdec, kdec, sdec, bd = tables
    return pl.pallas_call(
        _ret_kernel,
        grid=(batch, nseq),
        in_specs=[
            pl.BlockSpec((tr, RQK_PAD), row),
            pl.BlockSpec((tr, RQK_PAD), row),
            pl.BlockSpec((tr, RET_W), row),
            pl.BlockSpec((tr, RET_W), row),
            _const_spec(decay.shape),
            _const_spec(qdec.shape),
            _const_spec(kdec.shape),
            _const_spec(sdec.shape),
            _const_spec(bd.shape),
            _const_spec((1, RET_W)),
            _const_spec((RET_W, RET_W)),
        ],
        out_specs=pl.BlockSpec((tr, RET_W), row),
        out_shape=jax.ShapeDtypeStruct((batch * seq, RET_W), BF16),
        scratch_shapes=[pltpu.VMEM((RQK_PAD, RET_W), F32)],
        compiler_params=_params("arbitrary", "arbitrary"),
        name="retention",
    )(rq, rk, rv, rg, decay, qdec, kdec, sdec, bd, retg, ones384)


def _attn_kernel(q_ref, k_ref, v_ref, tbl_ref, ones_ref, out_ref,
                 kb1, vb1, kb4, vb4, kb16, vb16,
                 acc1, m1, s1, acc4, m4, s4, acc16, m16, s16):
    span_i = pl.program_id(2)
    L = ATT_BLOCK
    branches = ((1, kb1, vb1, acc1, m1, s1), (4, kb4, vb4, acc4, m4, s4), (16, kb16, vb16, acc16, m16, s16))

    @pl.when(span_i == 0)
    def _():
        for d, kb, vb, _, _, _ in branches:
            kb[0:L * d, :] = jnp.zeros((L * d, LANES), BF16)
            vb[0:L * d, :] = jnp.zeros((L * d, LANES), BF16)

    first_head = lax.broadcasted_iota(jnp.int32, (L, LANES), 1) < ATT_HDIM
    first_head2 = lax.broadcasted_iota(jnp.int32, (2 * L, LANES), 1) < ATT_HDIM

    for b, (d, kb, vb, acc_r, m_r, s_r) in enumerate(branches):
        prevlen = L * d
        shift = int(math.log2(d))

        def unit(u, carry, b=b, d=d, kb=kb, vb=vb, acc_r=acc_r, m_r=m_r, s_r=s_r, prevlen=prevlen, shift=shift):
            sub = lax.shift_right_logical(u, shift)
            res = u - lax.shift_left(sub, shift)
            off = pl.multiple_of(u * L, L)
            if d == 1:
                rows = pl.ds(off, L)
            else:
                rows = pl.ds(sub * (L * d) + res, L, stride=d)
            q = q_ref[rows, :].astype(BF16)
            kc = k_ref[rows, :].astype(BF16)
            vc = v_ref[rows, :].astype(BF16)
            kp = kb[pl.ds(off, L), :]
            vp = vb[pl.ds(off, L), :]
            kb[pl.ds(off + prevlen, L), :] = kc
            vb[pl.ds(off + prevlen, L), :] = vc
            kcat = jnp.concatenate([kp, kc], axis=0)
            vcat = jnp.concatenate([vp, vc], axis=0)
            zq = jnp.zeros_like(q)
            q2 = jnp.concatenate([jnp.where(first_head, q, zq), jnp.where(first_head, zq, q)], axis=0)
            variant = jnp.logical_and(span_i == 0, sub == 0).astype(jnp.int32)
            logits = lax.dot_general(q2, kcat, (((1,), (1,)), ((), ())), preferred_element_type=F32)
            logits = logits + tbl_ref[b, variant]
            m = jnp.max(logits, axis=-1, keepdims=True)
            pb = jnp.exp2(logits - m).astype(BF16)
            pcat = jnp.concatenate([pb[:L], pb[L:]], axis=1)
            zv = jnp.zeros_like(vcat)
            vst = jnp.concatenate([jnp.where(first_head2, vcat, zv), jnp.where(first_head2, zv, vcat)], axis=0)
            acc = jnp.dot(pcat, jnp.concatenate([vst, ones_ref[...]], axis=1), preferred_element_type=F32)
            acc_r[rows, :] = acc[:, :LANES]
            s_r[rows, :] = acc[:, LANES:]
            m_r[rows, :] = jnp.where(first_head, m[:L], m[L:])
            return carry

        lax.fori_loop(0, SPAN // L, unit, 0, unroll=UNIT_UNROLL)

    CR = 256

    def combine(c, carry):
        rows = pl.ds(pl.multiple_of(c * CR, CR), CR)
        ma, mb, mc = m1[rows, :], m4[rows, :], m16[rows, :]
        mx = jnp.maximum(jnp.maximum(ma, mb), mc)
        ea, eb, ec = jnp.exp2(ma - mx), jnp.exp2(mb - mx), jnp.exp2(mc - mx)
        num = ea * acc1[rows, :] + eb * acc4[rows, :] + ec * acc16[rows, :]
        den = ea * s1[rows, :] + eb * s4[rows, :] + ec * s16[rows, :]
        out_ref[rows, :] = (num / den).astype(BF16)
        return carry

    lax.fori_loop(0, SPAN // CR, combine, 0)

    for d, kb, vb, _, _, _ in branches:
        kb[0:L * d, :] = kb[SPAN:SPAN + L * d, :]
        vb[0:L * d, :] = vb[SPAN:SPAN + L * d, :]


def _rowsum_ones():
    row_first = jnp.arange(4 * ATT_BLOCK)[:, None] < 2 * ATT_BLOCK
    lane_first = jnp.arange(LANES)[None, :] < ATT_HDIM
    return (row_first == lane_first).astype(BF16)


def _attention(aq, ak, av, tbl, batch, seq):
    nslab = ATT_W // LANES
    nspan = seq // SPAN
    blk = lambda b, p, i: (p, b * nspan + i, 0)
    scratch = []
    for d in DILATIONS:
        scratch += [pltpu.VMEM((ATT_BLOCK * d + SPAN, LANES), BF16)] * 2
    scratch += [pltpu.VMEM((SPAN, LANES), F32)] * 9
    return pl.pallas_call(
        _attn_kernel,
        grid=(batch, nslab, nspan),
        in_specs=[
            pl.BlockSpec((None, SPAN, LANES), blk),
            pl.BlockSpec((None, SPAN, LANES), blk),
            pl.BlockSpec((None, SPAN, LANES), blk),
            pl.BlockSpec((None, len(DILATIONS), 2, 2 * ATT_BLOCK, 2 * ATT_BLOCK), lambda b, p, i: (p, 0, 0, 0, 0)),
            _const_spec((4 * ATT_BLOCK, LANES)),
        ],
        out_specs=pl.BlockSpec((None, SPAN, LANES), blk),
        out_shape=jax.ShapeDtypeStruct((nslab, batch * seq, LANES), BF16),
        scratch_shapes=scratch,
        compiler_params=_params("arbitrary", "arbitrary", "arbitrary"),
        name="dilated_attn",
    )(aq, ak, av, tbl, _rowsum_ones())


def _out_ffn_kernel(x_ref, conv_ref, ret_ref, att_ref, wout_ref, g2_ref, w1_ref, w2_ref, o_ref):
    mix = jnp.concatenate([conv_ref[...], ret_ref[...]] + [att_ref[p] for p in range(ATT_W // LANES)], axis=1)
    x1 = x_ref[...] + jnp.dot(mix, wout_ref[...], preferred_element_type=F32)
    h2 = (x1 * lax.rsqrt(jnp.mean(x1 * x1, axis=-1, keepdims=True) + EPS) * g2_ref[...]).astype(BF16)
    acc = x1
    for c in range(D_FF // FF_CHUNK):
        sl = slice(c * FF_CHUNK, (c + 1) * FF_CHUNK)
        t = jnp.maximum(jnp.dot(h2, w1_ref[:, sl], preferred_element_type=F32), 0.0)
        acc = acc + jnp.dot((t * t).astype(BF16), w2_ref[sl, :], preferred_element_type=F32)
    o_ref[...] = acc


def _out_ffn(x2, conv_o, ret_o, att_o, w_out, g2, w1, w2):
    T = x2.shape[0]
    tm = TM_FFN
    row = lambda i: (i, 0)
    return pl.pallas_call(
        _out_ffn_kernel,
        grid=(T // tm,),
        in_specs=[
            pl.BlockSpec((tm, D_MODEL), row),
            pl.BlockSpec((tm, CONV_CH), row),
            pl.BlockSpec((tm, RET_W), row),
            pl.BlockSpec((ATT_W // LANES, tm, LANES), lambda i: (0, i, 0)),
            _const_spec((D_MODEL, D_MODEL)),
            _const_spec((1, D_MODEL)),
            _const_spec((D_MODEL, D_FF)),
            _const_spec((D_FF, D_MODEL)),
        ],
        out_specs=pl.BlockSpec((tm, D_MODEL), row),
        out_shape=jax.ShapeDtypeStruct((T, D_MODEL), F32),
        compiler_params=_params("arbitrary"),
        name="out_ffn",
    )(x2, conv_o, ret_o, att_o, w_out, g2, w1, w2)


def _rotary_tables(seq):
    half = RET_KDIM // 2
    inv = 1.0 / (ROPE_BASE ** jnp.linspace(0.0, 1.0, half, dtype=F32))
    ang = jnp.arange(seq, dtype=F32)[:, None] * inv[None, :]
    c, s = jnp.cos(ang), jnp.sin(ang)
    cos_h = jnp.repeat(c, 2, axis=1)
    sin_h = jnp.stack([-s, s], axis=-1).reshape(seq, RET_KDIM)
    pad = jnp.zeros((seq, RQK_PAD - RET_HEADS * RET_KDIM), F32)
    cos_t = jnp.concatenate([jnp.tile(cos_h, (1, RET_HEADS)), pad], axis=1)
    sin_t = jnp.concatenate([jnp.tile(sin_h, (1, RET_HEADS)), pad], axis=1)
    return cos_t, sin_t


def _retention_tables():
    C = RET_CHUNK
    H = RET_HEADS
    scale = RET_KDIM ** -0.5
    log_gamma = jnp.log(1.0 - 2.0 ** (-5.0 - jnp.arange(H, dtype=F32)))
    idx = jnp.arange(C, dtype=F32)
    diff = idx[:, None] - idx[None, :]
    decay = jnp.where(diff >= 0, jnp.exp(log_gamma[:, None, None] * jnp.maximum(diff, 0.0)), 0.0)
    decay = (decay * scale).reshape(H * C, C)
    lane_head = jnp.arange(RET_W) // RET_VDIM
    qdec = jnp.exp(log_gamma[lane_head][None, :] * (idx[:, None] + 1.0))
    kdec = jnp.exp(log_gamma[lane_head][None, :] * (C - 1.0 - idx[:, None])) * scale
    row_head = jnp.arange(RQK_PAD) // RET_KDIM
    bd = (row_head[:, None] == lane_head[None, :]).astype(F32)
    sdec = bd * jnp.exp(log_gamma * C)[lane_head][None, :]
    return decay, qdec, kdec, sdec, bd


def _t5_bucket(dist):
    max_exact = N_BUCKETS // 2
    nf = jnp.maximum(dist, 1).astype(F32)
    large = max_exact + (jnp.log(nf / max_exact) / math.log(MAX_DISTANCE / max_exact)
                         * (N_BUCKETS - max_exact)).astype(jnp.int32)
    large = jnp.minimum(large, N_BUCKETS - 1)
    return jnp.where(dist < max_exact, dist, large)


def _attention_tables(rel_bias):
    Lb = ATT_BLOCK
    qi = jnp.arange(Lb, dtype=jnp.int32)[:, None]
    kj = jnp.arange(2 * Lb, dtype=jnp.int32)[None, :]
    dist = Lb + qi - kj
    band = (dist >= 0) & (dist <= Lb)
    per_branch = []
    for d in DILATIONS:
        bucket = _t5_bucket(jnp.clip(dist, 0, Lb) * d)
        rb = rel_bias.astype(F32)
        bias = jnp.zeros((ATT_HEADS, Lb, 2 * Lb), F32)
        for n in range(N_BUCKETS):
            bias = jnp.where((bucket == n)[None], rb[n][:, None, None], bias)
        bias = bias * LOG2_E
        full = jnp.where(band[None], bias, NEG_INF)
        first = jnp.where((band & (kj >= Lb))[None], bias, NEG_INF)
        per_branch.append(jnp.stack([full, first], axis=0))
    t = jnp.stack(per_branch, axis=0)
    nslab = ATT_W // LANES
    t = t.reshape(len(DILATIONS), 2, nslab, 2 * Lb, 2 * Lb)
    return t.transpose(2, 0, 1, 3, 4)


def _pad_w_in(w_in):
    splits = [sum(_IN_SIZES[:i + 1]) for i in range(len(_IN_SIZES) - 1)]
    parts = jnp.split(w_in, splits, axis=1)
    zpad = jnp.zeros((w_in.shape[0], RQK_PAD - RET_HEADS * RET_KDIM), w_in.dtype)
    return jnp.concatenate([parts[0], parts[1], zpad, parts[2], zpad] + parts[3:], axis=1)


def _block_ones(n, width):
    g = jnp.arange(n) // width
    return (g[:, None] == g[None, :]).astype(BF16)


def kernel(x, norm1_g, w_in, conv_w, conv_b, conv_g, ret_g, q_g, k_g, w_out, norm2_g, w_ff1, w_ff2, rel_bias):
    batch, seq, _ = x.shape
    depth = w_in.shape[0]
    assert seq % SPAN == 0 and seq % TS_CONV == 0 and seq % TR_RET == 0 and seq % TM_PROJ == 0
    assert (batch * seq) % TM_FFN == 0

    cos_t, sin_t = _rotary_tables(seq)
    ret_tables = _retention_tables()
    att_tbl = _attention_tables(rel_bias)
    ones384 = _block_ones(ATT_W, ATT_HDIM)
    ones256 = _block_ones(CONV_CH, CONV_GROUP)

    x2 = x.reshape(batch * seq, D_MODEL)
    for l in range(depth):
        w_in_l = _pad_w_in(w_in[l]).astype(BF16)
        qg = jnp.tile(q_g[l].astype(F32), ATT_HEADS)[None, :] * (ATT_HDIM ** -0.5 * LOG2_E)
        kg = jnp.tile(k_g[l].astype(F32), ATT_HEADS)[None, :]
        hc, rq, rk, rv, rg, aq, ak, av = _in_proj(
            x2, norm1_g[l].astype(F32)[None, :], w_in_l, cos_t, sin_t, qg, kg, ones384, seq)
        conv_o = _conv(hc, conv_w[l].astype(F32), conv_b[l].astype(F32)[None, :],
                       conv_g[l].astype(F32)[None, :], ones256, batch, seq)
        ret_o = _retention(rq, rk, rv, rg, ret_tables, ret_g[l].astype(F32)[None, :], ones384, batch, seq)
        att_o = _attention(aq, ak, av, att_tbl, batch, seq)
        x2 = _out_ffn(x2, conv_o, ret_o, att_o, w_out[l].astype(BF16), norm2_g[l].astype(F32)[None, :],
                      w_ff1[l].astype(BF16), w_ff2[l].astype(BF16))
    return x2.reshape(batch, seq, D_MODEL)
```

```python
import functools
import math

import jax
import jax.numpy as jnp
from jax import lax
from jax.experimental import pallas as pl
from jax.experimental.pallas import tpu as pltpu

F32 = jnp.float32
BF16 = jnp.bfloat16

D_MODEL = 1024
CONV_CH = 256
SEG = 64
CONV_WIDTH = 31
RET_HEADS = 6
RET_KDIM = 32
RET_VDIM = 64
RET_W = RET_HEADS * RET_VDIM
ATT_HEADS = 6
ATT_HDIM = 64
ATT_W = ATT_HEADS * ATT_HDIM
DILATIONS = (1, 4, 16)
ATT_BLOCK = 128
N_BUCKETS = 32
MAX_DISTANCE = 2048
D_FF = 4 * D_MODEL
EPS = 1e-6
ROPE_BASE = 10000.0
NEG_INF = -1e30
LOG2_E = math.log2(math.e)

LANES = 128
RQK_PAD = 256
VMEM_LIMIT_BYTES = 56 * 1024 * 1024

_IN_SIZES = (2 * CONV_CH, RET_HEADS * RET_KDIM, RET_HEADS * RET_KDIM, RET_W, RET_W, ATT_W, ATT_W, ATT_W)
C_CONV = 0
C_RQ = 512
C_RK = C_RQ + RQK_PAD
C_RV = C_RK + RQK_PAD
C_RG = C_RV + RET_W
C_AQ = C_RG + RET_W
C_AK = C_AQ + ATT_W
C_AV = C_AK + ATT_W
IN_W_PAD = C_AV + ATT_W

TM_PROJ = 512
TM_FFN = 512
TS_CONV = 1024
CONV_ROWS = 64
CONV_UNROLL = 4
CONV_HALO = 32
TR_RET = 512
RET_CHUNK = 128
SPAN = ATT_BLOCK * DILATIONS[-1]
FF_CHUNK = 1024
UNIT_UNROLL = 8


def _const_spec(shape):
    nd = len(shape)
    return pl.BlockSpec(shape, lambda *_: (0,) * nd, pipeline_mode=pl.Buffered(1))


def _params(*sem):
    return pltpu.CompilerParams(dimension_semantics=sem, vmem_limit_bytes=VMEM_LIMIT_BYTES)


def _segment_mean(y, ones2_ref):
    ones2 = ones2_ref[...]
    parts = []
    for c in range(y.shape[1] // LANES):
        t = y[:, c * LANES:(c + 1) * LANES]
        hi = t.astype(BF16)
        lo = (t - hi.astype(F32)).astype(BF16)
        parts.append(jnp.dot(jnp.concatenate([hi, lo], axis=1), ones2, preferred_element_type=F32))
    return parts[0] if len(parts) == 1 else jnp.concatenate(parts, axis=1)


def _in_proj_kernel(x_ref, g1_ref, w_ref, cos_ref, sin_ref, qg_ref, kg_ref, ones2_ref,
                    hc_ref, rq_ref, rk_ref, rv_ref, rg_ref, aq_ref, ak_ref, av_ref):
    x = x_ref[...]
    h = x * lax.rsqrt(jnp.mean(x * x, axis=-1, keepdims=True) + EPS) * g1_ref[...]
    hb = h.astype(BF16)

    def proj(lo, hi):
        return jnp.dot(hb, w_ref[:, lo:hi], preferred_element_type=F32)

    u = proj(C_CONV, C_RV)
    hc_ref[...] = u[:, :CONV_CH] * jax.nn.sigmoid(u[:, CONV_CH:2 * CONV_CH])

    even = (lax.broadcasted_iota(jnp.int32, (1, LANES), 1) % 2) == 0
    cos = cos_ref[...]
    sin = sin_ref[...]
    for base, dst in ((C_RQ, rq_ref), (C_RK, rk_ref)):
        for c in range(RQK_PAD // LANES):
            t = u[:, base + c * LANES: base + (c + 1) * LANES]
            sw = jnp.where(even, pltpu.roll(t, LANES - 1, 1), pltpu.roll(t, 1, 1))
            dst[:, c * LANES:(c + 1) * LANES] = (t * cos + sw * sin).astype(BF16)

    r = proj(C_RV, C_AQ)
    rv_ref[...] = r[:, :RET_W].astype(BF16)
    g = r[:, RET_W:]
    rg_ref[...] = (g * jax.nn.sigmoid(g)).astype(BF16)

    a = proj(C_AQ, IN_W_PAD)
    aq = a[:, :ATT_W]
    aq = aq * lax.rsqrt(_segment_mean(aq * aq, ones2_ref) + EPS) * qg_ref[...]
    ak = a[:, ATT_W:2 * ATT_W]
    ak = ak * lax.rsqrt(_segment_mean(ak * ak, ones2_ref) + EPS) * kg_ref[...]
    av = a[:, 2 * ATT_W:]
    for p in range(ATT_W // LANES):
        sl = slice(p * LANES, (p + 1) * LANES)
        aq_ref[p] = aq[:, sl]
        ak_ref[p] = ak[:, sl]
        av_ref[p] = av[:, sl]


def _in_proj(x2, g1, w_in, cos_t, sin_t, qg, kg, ones2, seq):
    T = x2.shape[0]
    tm = TM_PROJ
    nseq = seq // tm
    row = lambda i: (i, 0)
    pos = lambda i: (i % nseq, 0)
    slab = lambda i: (0, i, 0)
    return pl.pallas_call(
        _in_proj_kernel,
        grid=(T // tm,),
        in_specs=[
            pl.BlockSpec((tm, D_MODEL), row),
            _const_spec((1, D_MODEL)),
            _const_spec((D_MODEL, IN_W_PAD)),
            pl.BlockSpec((tm, LANES), pos),
            pl.BlockSpec((tm, LANES), pos),
            _const_spec((1, ATT_W)),
            _const_spec((1, ATT_W)),
            _const_spec((2 * LANES, LANES)),
        ],
        out_specs=[
            pl.BlockSpec((tm, CONV_CH), row),
            pl.BlockSpec((tm, RQK_PAD), row),
            pl.BlockSpec((tm, RQK_PAD), row),
            pl.BlockSpec((tm, RET_W), row),
            pl.BlockSpec((tm, RET_W), row),
            pl.BlockSpec((ATT_W // LANES, tm, LANES), slab),
            pl.BlockSpec((ATT_W // LANES, tm, LANES), slab),
            pl.BlockSpec((ATT_W // LANES, tm, LANES), slab),
        ],
        out_shape=[
            jax.ShapeDtypeStruct((T, CONV_CH), F32),
            jax.ShapeDtypeStruct((T, RQK_PAD), BF16),
            jax.ShapeDtypeStruct((T, RQK_PAD), BF16),
            jax.ShapeDtypeStruct((T, RET_W), BF16),
            jax.ShapeDtypeStruct((T, RET_W), BF16),
            jax.ShapeDtypeStruct((ATT_W // LANES, T, LANES), F32),
            jax.ShapeDtypeStruct((ATT_W // LANES, T, LANES), F32),
            jax.ShapeDtypeStruct((ATT_W // LANES, T, LANES), F32),
        ],
        compiler_params=_params("arbitrary"),
        name="in_proj",
    )(x2, g1, w_in, cos_t, sin_t, qg, kg, ones2)


def _conv_kernel(hc_ref, cw_ref, cb_ref, cg_ref, ones2_ref, out_ref, ext_ref):
    ts = hc_ref.shape[0]
    nslab = CONV_CH // LANES

    @pl.when(pl.program_id(1) == 0)
    def _():
        ext_ref[:, 0:CONV_HALO, :] = jnp.zeros((nslab, CONV_HALO, LANES), F32)

    for j in range(nslab):
        ext_ref[j, CONV_HALO:CONV_HALO + ts, :] = hc_ref[:, j * LANES:(j + 1) * LANES]

    first_tap = CONV_HALO - (CONV_WIDTH - 1)

    def chunk(c, carry):
        r0 = pl.multiple_of(c * CONV_ROWS, CONV_ROWS)
        parts = []
        for j in range(nslab):
            sl = slice(j * LANES, (j + 1) * LANES)
            acc = jnp.broadcast_to(cb_ref[:, sl], (CONV_ROWS, LANES))
            for w in range(CONV_WIDTH):
                acc = acc + ext_ref[j, pl.ds(r0 + first_tap + w, CONV_ROWS), :] * cw_ref[w:w + 1, sl]
            parts.append(acc)
        hcv = jnp.concatenate(parts, axis=1)
        ms = _segment_mean(hcv * hcv, ones2_ref)
        y = hcv * lax.rsqrt(ms + EPS) * cg_ref[...]
        out_ref[pl.ds(r0, CONV_ROWS), :] = (y * jax.nn.sigmoid(y)).astype(BF16)
        return carry

    lax.fori_loop(0, ts // CONV_ROWS, chunk, 0, unroll=CONV_UNROLL)
    ext_ref[:, 0:CONV_HALO, :] = ext_ref[:, ts:ts + CONV_HALO, :]


def _conv(hc, cw, cb, cg, ones2, batch, seq):
    ts = TS_CONV
    nseq = seq // ts
    row = lambda b, i: (b * nseq + i, 0)
    return pl.pallas_call(
        _conv_kernel,
        grid=(batch, nseq),
        in_specs=[
            pl.BlockSpec((ts, CONV_CH), row),
            _const_spec((CONV_WIDTH, CONV_CH)),
            _const_spec((1, CONV_CH)),
            _const_spec((1, CONV_CH)),
            _const_spec((2 * LANES, LANES)),
        ],
        out_specs=pl.BlockSpec((ts, CONV_CH), row),
        out_shape=jax.ShapeDtypeStruct((batch * seq, CONV_CH), BF16),
        scratch_shapes=[pltpu.VMEM((CONV_CH // LANES, CONV_HALO + ts, LANES), F32)],
        compiler_params=_params("arbitrary", "arbitrary"),
        name="conv",
    )(hc, cw, cb, cg, ones2)


def _ret_kernel(rq_ref, rk_ref, rv_ref, rg_ref, decay_ref, qdec_ref, kdec_ref, sdec_ref, bd_ref,
                retg_ref, ones2_ref, out_ref, state_ref):
    @pl.when(pl.program_id(1) == 0)
    def _():
        state_ref[...] = jnp.zeros(state_ref.shape, F32)

    C = RET_CHUNK
    qlane_head = lax.broadcasted_iota(jnp.int32, (C, RQK_PAD), 1) // RET_KDIM
    vlane_first = (lax.broadcasted_iota(jnp.int32, (C, LANES), 1) // RET_VDIM) == 0
    npair = RET_W // LANES

    for c in range(rq_ref.shape[0] // C):
        rows = slice(c * C, (c + 1) * C)
        q = rq_ref[rows, :]
        k = rk_ref[rows, :]
        v = rv_ref[rows, :]

        zq = jnp.zeros_like(q)
        qs = jnp.concatenate([jnp.where(qlane_head == h, q, zq) for h in range(RET_HEADS)], axis=0)
        sc = lax.dot_general(qs, k, (((1,), (1,)), ((), ())), preferred_element_type=F32)
        p = (sc * decay_ref[...]).astype(BF16)

        inner = []
        for pr in range(npair):
            vp = v[:, pr * LANES:(pr + 1) * LANES]
            zv = jnp.zeros_like(vp)
            pcat = jnp.concatenate([p[(2 * pr) * C:(2 * pr + 1) * C], p[(2 * pr + 1) * C:(2 * pr + 2) * C]], axis=1)
            vst = jnp.concatenate([jnp.where(vlane_first, vp, zv), jnp.where(vlane_first, zv, vp)], axis=0)
            inner.append(jnp.dot(pcat, vst, preferred_element_type=F32))
        inner = jnp.concatenate(inner, axis=1)

        state = state_ref[...]
        cross = jnp.dot(q, state.astype(BF16), preferred_element_type=F32) * qdec_ref[...]
        vdec = (v.astype(F32) * kdec_ref[...]).astype(BF16)
        kv = lax.dot_general(k, vdec, (((0,), (0,)), ((), ())), preferred_element_type=F32)
        state_ref[...] = state * sdec_ref[...] + kv * bd_ref[...]

        o = inner + cross
        o = o * lax.rsqrt(_segment_mean(o * o, ones2_ref) + EPS) * retg_ref[...]
        out_ref[rows, :] = (o * rg_ref[rows, :].astype(F32)).astype(BF16)


def _retention(rq, rk, rv, rg, tables, retg, ones2, batch, seq):
    tr = TR_RET
    nseq = seq // tr
    row = lambda b, i: (b * nseq + i, 0)
    decay, qdec, kdec, sdec, bd = tables
    return pl.pallas_call(
        _ret_kernel,
        grid=(batch, nseq),
        in_specs=[
            pl.BlockSpec((tr, RQK_PAD), row),
            pl.BlockSpec((tr, RQK_PAD), row),
            pl.BlockSpec((tr, RET_W), row),
            pl.BlockSpec((tr, RET_W), row),
            _const_spec(decay.shape),
            _const_spec(qdec.shape),
            _const_spec(kdec.shape),
            _const_spec(sdec.shape),
            _const_spec(bd.shape),
            _const_spec((1, RET_W)),
            _const_spec((2 * LANES, LANES)),
        ],
        out_specs=pl.BlockSpec((tr, RET_W), row),
        out_shape=jax.ShapeDtypeStruct((batch * seq, RET_W), BF16),
        scratch_shapes=[pltpu.VMEM((RQK_PAD, RET_W), F32)],
        compiler_params=_params("arbitrary", "arbitrary"),
        name="retention",
    )(rq, rk, rv, rg, decay, qdec, kdec, sdec, bd, retg, ones2)


def _attn_kernel(q_ref, k_ref, v_ref, tbl_ref, ones_ref, out_ref,
                 kb1, vb1, kb4, vb4, kb16, vb16,
                 acc1, m1, s1, acc4, m4, s4, acc16, m16, s16):
    span_i = pl.program_id(2)
    L = ATT_BLOCK
    branches = ((1, kb1, vb1, acc1, m1, s1), (4, kb4, vb4, acc4, m4, s4), (16, kb16, vb16, acc16, m16, s16))

    @pl.when(span_i == 0)
    def _():
        for d, kb, vb, _, _, _ in branches:
            kb[0:L * d, :] = jnp.zeros((L * d, LANES), BF16)
            vb[0:L * d, :] = jnp.zeros((L * d, LANES), BF16)

    first_head = lax.broadcasted_iota(jnp.int32, (L, LANES), 1) < ATT_HDIM
    first_head2 = lax.broadcasted_iota(jnp.int32, (2 * L, LANES), 1) < ATT_HDIM

    for b, (d, kb, vb, acc_r, m_r, s_r) in enumerate(branches):
        prevlen = L * d
        shift = int(math.log2(d))

        def unit(u, carry, b=b, d=d, kb=kb, vb=vb, acc_r=acc_r, m_r=m_r, s_r=s_r, prevlen=prevlen, shift=shift):
            sub = lax.shift_right_logical(u, shift)
            res = u - lax.shift_left(sub, shift)
            off = pl.multiple_of(u * L, L)
            if d == 1:
                rows = pl.ds(off, L)
            else:
                rows = pl.ds(sub * (L * d) + res, L, stride=d)
            q = q_ref[rows, :].astype(BF16)
            kc = k_ref[rows, :].astype(BF16)
            vc = v_ref[rows, :].astype(BF16)
            kp = kb[pl.ds(off, L), :]
            vp = vb[pl.ds(off, L), :]
            kb[pl.ds(off + prevlen, L), :] = kc
            vb[pl.ds(off + prevlen, L), :] = vc
            kcat = jnp.concatenate([kp, kc], axis=0)
            vcat = jnp.concatenate([vp, vc], axis=0)
            zq = jnp.zeros_like(q)
            q2 = jnp.concatenate([jnp.where(first_head, q, zq), jnp.where(first_head, zq, q)], axis=0)
            variant = jnp.logical_and(span_i == 0, sub == 0).astype(jnp.int32)
            logits = lax.dot_general(q2, kcat, (((1,), (1,)), ((), ())), preferred_element_type=F32)
            logits = logits + tbl_ref[b, variant]
            m = jnp.max(logits, axis=-1, keepdims=True)
            pb = jnp.exp2(logits - m).astype(BF16)
            pcat = jnp.concatenate([pb[:L], pb[L:]], axis=1)
            zv = jnp.zeros_like(vcat)
            vst = jnp.concatenate([jnp.where(first_head2, vcat, zv), jnp.where(first_head2, zv, vcat)], axis=0)
            acc = jnp.dot(pcat, jnp.concatenate([vst, ones_ref[...]], axis=1), preferred_element_type=F32)
            acc_r[rows, :] = acc[:, :LANES]
            s_r[rows, :] = acc[:, LANES:]
            m_r[rows, :] = jnp.where(first_head, m[:L], m[L:])
            return carry

        lax.fori_loop(0, SPAN // L, unit, 0, unroll=UNIT_UNROLL)

    CR = 256

    def combine(c, carry):
        rows = pl.ds(pl.multiple_of(c * CR, CR), CR)
        ma, mb, mc = m1[rows, :], m4[rows, :], m16[rows, :]
        mx = jnp.maximum(jnp.maximum(ma, mb), mc)
        ea, eb, ec = jnp.exp2(ma - mx), jnp.exp2(mb - mx), jnp.exp2(mc - mx)
        num = ea * acc1[rows, :] + eb * acc4[rows, :] + ec * acc16[rows, :]
        den = ea * s1[rows, :] + eb * s4[rows, :] + ec * s16[rows, :]
        out_ref[rows, :] = (num / den).astype(BF16)
        return carry

    lax.fori_loop(0, SPAN // CR, combine, 0)

    for d, kb, vb, _, _, _ in branches:
        kb[0:L * d, :] = kb[SPAN:SPAN + L * d, :]
        vb[0:L * d, :] = vb[SPAN:SPAN + L * d, :]


def _rowsum_ones():
    row_first = jnp.arange(4 * ATT_BLOCK)[:, None] < 2 * ATT_BLOCK
    lane_first = jnp.arange(LANES)[None, :] < ATT_HDIM
    return (row_first == lane_first).astype(BF16)


def _attention(aq, ak, av, tbl, batch, seq):
    nslab = ATT_W // LANES
    nspan = seq // SPAN
    blk = lambda b, p, i: (p, b * nspan + i, 0)
    scratch = []
    for d in DILATIONS:
        scratch += [pltpu.VMEM((ATT_BLOCK * d + SPAN, LANES), BF16)] * 2
    scratch += [pltpu.VMEM((SPAN, LANES), F32)] * 9
    return pl.pallas_call(
        _attn_kernel,
        grid=(batch, nslab, nspan),
        in_specs=[
            pl.BlockSpec((None, SPAN, LANES), blk),
            pl.BlockSpec((None, SPAN, LANES), blk),
            pl.BlockSpec((None, SPAN, LANES), blk),
            pl.BlockSpec((None, len(DILATIONS), 2, 2 * ATT_BLOCK, 2 * ATT_BLOCK), lambda b, p, i: (p, 0, 0, 0, 0)),
            _const_spec((4 * ATT_BLOCK, LANES)),
        ],
        out_specs=pl.BlockSpec((None, SPAN, LANES), blk),
        out_shape=jax.ShapeDtypeStruct((nslab, batch * seq, LANES), BF16),
        scratch_shapes=scratch,
        compiler_params=_params("arbitrary", "arbitrary", "arbitrary"),
        name="dilated_attn",
    )(aq, ak, av, tbl, _rowsum_ones())


def _out_ffn_kernel(x_ref, conv_ref, ret_ref, att_ref, wout_ref, g2_ref, w1_ref, w2_ref, o_ref):
    mix = jnp.concatenate([conv_ref[...], ret_ref[...]] + [att_ref[p] for p in range(ATT_W // LANES)], axis=1)
    x1 = x_ref[...] + jnp.dot(mix, wout_ref[...], preferred_element_type=F32)
    h2 = (x1 * lax.rsqrt(jnp.mean(x1 * x1, axis=-1, keepdims=True) + EPS) * g2_ref[...]).astype(BF16)
    acc = x1
    for c in range(D_FF // FF_CHUNK):
        sl = slice(c * FF_CHUNK, (c + 1) * FF_CHUNK)
        t = jnp.maximum(jnp.dot(h2, w1_ref[:, sl], preferred_element_type=F32), 0.0)
        acc = acc + jnp.dot((t * t).astype(BF16), w2_ref[sl, :], preferred_element_type=F32)
    o_ref[...] = acc


def _out_ffn(x2, conv_o, ret_o, att_o, w_out, g2, w1, w2):
    T = x2.shape[0]
    tm = TM_FFN
    row = lambda i: (i, 0)
    return pl.pallas_call(
        _out_ffn_kernel,
        grid=(T // tm,),
        in_specs=[
            pl.BlockSpec((tm, D_MODEL), row),
            pl.BlockSpec((tm, CONV_CH), row),
            pl.BlockSpec((tm, RET_W), row),
            pl.BlockSpec((ATT_W // LANES, tm, LANES), lambda i: (0, i, 0)),
            _const_spec((D_MODEL, D_MODEL)),
            _const_spec((1, D_MODEL)),
            _const_spec((D_MODEL, D_FF)),
            _const_spec((D_FF, D_MODEL)),
        ],
        out_specs=pl.BlockSpec((tm, D_MODEL), row),
        out_shape=jax.ShapeDtypeStruct((T, D_MODEL), F32),
        compiler_params=_params("arbitrary"),
        name="out_ffn",
    )(x2, conv_o, ret_o, att_o, w_out, g2, w1, w2)


def _rotary_tables(seq):
    half = RET_KDIM // 2
    inv = 1.0 / (ROPE_BASE ** jnp.linspace(0.0, 1.0, half, dtype=F32))
    ang = jnp.arange(seq, dtype=F32)[:, None] * inv[None, :]
    c, s = jnp.cos(ang), jnp.sin(ang)
    cos_h = jnp.repeat(c, 2, axis=1)
    sin_h = jnp.stack([-s, s], axis=-1).reshape(seq, RET_KDIM)
    reps = LANES // RET_KDIM
    return jnp.tile(cos_h, (1, reps)), jnp.tile(sin_h, (1, reps))


def _retention_tables():
    C = RET_CHUNK
    H = RET_HEADS
    scale = RET_KDIM ** -0.5
    log_gamma = jnp.log(1.0 - 2.0 ** (-5.0 - jnp.arange(H, dtype=F32)))
    idx = jnp.arange(C, dtype=F32)
    diff = idx[:, None] - idx[None, :]
    decay = jnp.where(diff >= 0, jnp.exp(log_gamma[:, None, None] * jnp.maximum(diff, 0.0)), 0.0)
    decay = (decay * scale).reshape(H * C, C)
    lane_head = jnp.arange(RET_W) // RET_VDIM
    qdec = jnp.exp(log_gamma[lane_head][None, :] * (idx[:, None] + 1.0))
    kdec = jnp.exp(log_gamma[lane_head][None, :] * (C - 1.0 - idx[:, None])) * scale
    row_head = jnp.arange(RQK_PAD) // RET_KDIM
    bd = (row_head[:, None] == lane_head[None, :]).astype(F32)
    sdec = bd * jnp.exp(log_gamma * C)[lane_head][None, :]
    return decay, qdec, kdec, sdec, bd


def _t5_bucket(dist):
    max_exact = N_BUCKETS // 2
    nf = jnp.maximum(dist, 1).astype(F32)
    large = max_exact + (jnp.log(nf / max_exact) / math.log(MAX_DISTANCE / max_exact)
                         * (N_BUCKETS - max_exact)).astype(jnp.int32)
    large = jnp.minimum(large, N_BUCKETS - 1)
    return jnp.where(dist < max_exact, dist, large)


def _attention_tables(rel_bias):
    Lb = ATT_BLOCK
    qi = jnp.arange(Lb, dtype=jnp.int32)[:, None]
    kj = jnp.arange(2 * Lb, dtype=jnp.int32)[None, :]
    dist = Lb + qi - kj
    band = (dist >= 0) & (dist <= Lb)
    per_branch = []
    for d in DILATIONS:
        bucket = _t5_bucket(jnp.clip(dist, 0, Lb) * d)
        rb = rel_bias.astype(F32)
        bias = jnp.zeros((ATT_HEADS, Lb, 2 * Lb), F32)
        for n in range(N_BUCKETS):
            bias = jnp.where((bucket == n)[None], rb[n][:, None, None], bias)
        bias = bias * LOG2_E
        full = jnp.where(band[None], bias, NEG_INF)
        first = jnp.where((band & (kj >= Lb))[None], bias, NEG_INF)
        per_branch.append(jnp.stack([full, first], axis=0))
    t = jnp.stack(per_branch, axis=0)
    nslab = ATT_W // LANES
    t = t.reshape(len(DILATIONS), 2, nslab, 2 * Lb, 2 * Lb)
    return t.transpose(2, 0, 1, 3, 4)


def _pad_w_in(w_in):
    splits = [sum(_IN_SIZES[:i + 1]) for i in range(len(_IN_SIZES) - 1)]
    parts = jnp.split(w_in, splits, axis=1)
    zpad = jnp.zeros((w_in.shape[0], RQK_PAD - RET_HEADS * RET_KDIM), w_in.dtype)
    return jnp.concatenate([parts[0], parts[1], zpad, parts[2], zpad] + parts[3:], axis=1)


def _stacked_block_ones():
    g = jnp.arange(LANES) // SEG
    blk = ((g[:, None] == g[None, :]).astype(F32) * (1.0 / SEG)).astype(BF16)
    return jnp.concatenate([blk, blk], axis=0)


def kernel(x, norm1_g, w_in, conv_w, conv_b, conv_g, ret_g, q_g, k_g, w_out, norm2_g, w_ff1, w_ff2, rel_bias):
    batch, seq, _ = x.shape
    depth = w_in.shape[0]
    assert seq % SPAN == 0 and seq % TS_CONV == 0 and seq % TR_RET == 0 and seq % TM_PROJ == 0
    assert (batch * seq) % TM_FFN == 0

    cos_t, sin_t = _rotary_tables(seq)
    ret_tables = _retention_tables()
    att_tbl = _attention_tables(rel_bias)
    ones2 = _stacked_block_ones()

    x2 = x.reshape(batch * seq, D_MODEL)
    for l in range(depth):
        w_in_l = _pad_w_in(w_in[l]).astype(BF16)
        qg = jnp.tile(q_g[l].astype(F32), ATT_HEADS)[None, :] * (ATT_HDIM ** -0.5 * LOG2_E)
        kg = jnp.tile(k_g[l].astype(F32), ATT_HEADS)[None, :]
        hc, rq, rk, rv, rg, aq, ak, av = _in_proj(
            x2, norm1_g[l].astype(F32)[None, :], w_in_l, cos_t, sin_t, qg, kg, ones2, seq)
        conv_o = _conv(hc, conv_w[l].astype(F32), conv_b[l].astype(F32)[None, :],
                       conv_g[l].astype(F32)[None, :], ones2, batch, seq)
        ret_o = _retention(rq, rk, rv, rg, ret_tables, ret_g[l].astype(F32)[None, :], ones2, batch, seq)
        att_o = _attention(aq, ak, av, att_tbl, batch, seq)
        x2 = _out_ffn(x2, conv_o, ret_o, att_o, w_out[l].astype(BF16), norm2_g[l].astype(F32)[None, :],
                      w_ff1[l].astype(BF16), w_ff2[l].astype(BF16))
    return x2.reshape(batch, seq, D_MODEL)
```

```python
import functools
import math

import jax
import jax.numpy as jnp
from jax import lax
from jax.experimental import pallas as pl
from jax.experimental.pallas import tpu as pltpu

F32 = jnp.float32
BF16 = jnp.bfloat16

D_MODEL = 1024
CONV_CH = 256
SEG = 64
CONV_WIDTH = 31
RET_HEADS = 6
RET_KDIM = 32
RET_VDIM = 64
RET_W = RET_HEADS * RET_VDIM
ATT_HEADS = 6
ATT_HDIM = 64
ATT_W = ATT_HEADS * ATT_HDIM
DILATIONS = (1, 4, 16)
ATT_BLOCK = 128
N_BUCKETS = 32
MAX_DISTANCE = 2048
D_FF = 4 * D_MODEL
EPS = 1e-6
ROPE_BASE = 10000.0
NEG_INF = -1e30
LOG2_E = math.log2(math.e)

LANES = 128
RQK_PAD = 256
VMEM_LIMIT_BYTES = 56 * 1024 * 1024

_IN_SIZES = (2 * CONV_CH, RET_HEADS * RET_KDIM, RET_HEADS * RET_KDIM, RET_W, RET_W, ATT_W, ATT_W, ATT_W)
C_CONV = 0
C_RQ = 512
C_RK = C_RQ + RQK_PAD
C_RV = C_RK + RQK_PAD
C_RG = C_RV + RET_W
C_AQ = C_RG + RET_W
C_AK = C_AQ + ATT_W
C_AV = C_AK + ATT_W
IN_W_PAD = C_AV + ATT_W

TM_PROJ = 512
TM_FFN = 512
TS_CONV = 1024
CONV_ROWS = 64
CONV_UNROLL = 4
CONV_HALO = 32
TR_RET = 512
RET_CHUNK = 128
SPAN = ATT_BLOCK * DILATIONS[-1]
FF_CHUNK = 1024
UNIT_UNROLL = 16


def _const_spec(shape):
    nd = len(shape)
    return pl.BlockSpec(shape, lambda *_: (0,) * nd, pipeline_mode=pl.Buffered(1))


def _params(*sem):
    return pltpu.CompilerParams(dimension_semantics=sem, vmem_limit_bytes=VMEM_LIMIT_BYTES)


def _segment_mean(y, ones2_ref):
    ones2 = ones2_ref[...]
    parts = []
    for c in range(y.shape[1] // LANES):
        t = y[:, c * LANES:(c + 1) * LANES]
        hi = t.astype(BF16)
        lo = (t - hi.astype(F32)).astype(BF16)
        parts.append(jnp.dot(jnp.concatenate([hi, lo], axis=1), ones2, preferred_element_type=F32))
    return parts[0] if len(parts) == 1 else jnp.concatenate(parts, axis=1)


def _in_proj_kernel(x_ref, g1_ref, w_ref, cos_ref, sin_ref, qg_ref, kg_ref, ones2_ref,
                    hc_ref, rq_ref, rk_ref, rv_ref, rg_ref, aq_ref, ak_ref, av_ref):
    x = x_ref[...]
    h = x * lax.rsqrt(jnp.mean(x * x, axis=-1, keepdims=True) + EPS) * g1_ref[...]
    hb = h.astype(BF16)

    def proj(lo, hi):
        return jnp.dot(hb, w_ref[:, lo:hi], preferred_element_type=F32)

    u = proj(C_CONV, C_RV)
    hc_ref[...] = u[:, :CONV_CH] * jax.nn.sigmoid(u[:, CONV_CH:2 * CONV_CH])

    even = (lax.broadcasted_iota(jnp.int32, (1, LANES), 1) % 2) == 0
    cos = cos_ref[...]
    sin = sin_ref[...]
    for base, dst in ((C_RQ, rq_ref), (C_RK, rk_ref)):
        for c in range(RQK_PAD // LANES):
            t = u[:, base + c * LANES: base + (c + 1) * LANES]
            sw = jnp.where(even, pltpu.roll(t, LANES - 1, 1), pltpu.roll(t, 1, 1))
            dst[:, c * LANES:(c + 1) * LANES] = (t * cos + sw * sin).astype(BF16)

    r = proj(C_RV, C_AQ)
    rv_ref[...] = r[:, :RET_W].astype(BF16)
    g = r[:, RET_W:]
    rg_ref[...] = (g * jax.nn.sigmoid(g)).astype(BF16)

    a = proj(C_AQ, IN_W_PAD)
    aq = a[:, :ATT_W]
    aq = aq * lax.rsqrt(_segment_mean(aq * aq, ones2_ref) + EPS) * qg_ref[...]
    ak = a[:, ATT_W:2 * ATT_W]
    ak = ak * lax.rsqrt(_segment_mean(ak * ak, ones2_ref) + EPS) * kg_ref[...]
    av = a[:, 2 * ATT_W:]
    for p in range(ATT_W // LANES):
        sl = slice(p * LANES, (p + 1) * LANES)
        aq_ref[p] = aq[:, sl]
        ak_ref[p] = ak[:, sl]
        av_ref[p] = av[:, sl]


def _in_proj(x2, g1, w_in, cos_t, sin_t, qg, kg, ones2, seq):
    T = x2.shape[0]
    tm = TM_PROJ
    nseq = seq // tm
    row = lambda i: (i, 0)
    pos = lambda i: (i % nseq, 0)
    slab = lambda i: (0, i, 0)
    return pl.pallas_call(
        _in_proj_kernel,
        grid=(T // tm,),
        in_specs=[
            pl.BlockSpec((tm, D_MODEL), row),
            _const_spec((1, D_MODEL)),
            _const_spec((D_MODEL, IN_W_PAD)),
            pl.BlockSpec((tm, LANES), pos),
            pl.BlockSpec((tm, LANES), pos),
            _const_spec((1, ATT_W)),
            _const_spec((1, ATT_W)),
            _const_spec((2 * LANES, LANES)),
        ],
        out_specs=[
            pl.BlockSpec((tm, CONV_CH), row),
            pl.BlockSpec((tm, RQK_PAD), row),
            pl.BlockSpec((tm, RQK_PAD), row),
            pl.BlockSpec((tm, RET_W), row),
            pl.BlockSpec((tm, RET_W), row),
            pl.BlockSpec((ATT_W // LANES, tm, LANES), slab),
            pl.BlockSpec((ATT_W // LANES, tm, LANES), slab),
            pl.BlockSpec((ATT_W // LANES, tm, LANES), slab),
        ],
        out_shape=[
            jax.ShapeDtypeStruct((T, CONV_CH), F32),
            jax.ShapeDtypeStruct((T, RQK_PAD), BF16),
            jax.ShapeDtypeStruct((T, RQK_PAD), BF16),
            jax.ShapeDtypeStruct((T, RET_W), BF16),
            jax.ShapeDtypeStruct((T, RET_W), BF16),
            jax.ShapeDtypeStruct((ATT_W // LANES, T, LANES), F32),
            jax.ShapeDtypeStruct((ATT_W // LANES, T, LANES), F32),
            jax.ShapeDtypeStruct((ATT_W // LANES, T, LANES), F32),
        ],
        compiler_params=_params("arbitrary"),
        name="in_proj",
    )(x2, g1, w_in, cos_t, sin_t, qg, kg, ones2)


def _conv_kernel(hc_ref, cw_ref, cb_ref, cg_ref, ones2_ref, out_ref, ext_ref):
    ts = hc_ref.shape[0]
    nslab = CONV_CH // LANES

    @pl.when(pl.program_id(1) == 0)
    def _():
        ext_ref[:, 0:CONV_HALO, :] = jnp.zeros((nslab, CONV_HALO, LANES), F32)

    for j in range(nslab):
        ext_ref[j, CONV_HALO:CONV_HALO + ts, :] = hc_ref[:, j * LANES:(j + 1) * LANES]

    first_tap = CONV_HALO - (CONV_WIDTH - 1)

    def chunk(c, carry):
        r0 = pl.multiple_of(c * CONV_ROWS, CONV_ROWS)
        parts = []
        for j in range(nslab):
            sl = slice(j * LANES, (j + 1) * LANES)
            acc = jnp.broadcast_to(cb_ref[:, sl], (CONV_ROWS, LANES))
            for w in range(CONV_WIDTH):
                acc = acc + ext_ref[j, pl.ds(r0 + first_tap + w, CONV_ROWS), :] * cw_ref[w:w + 1, sl]
            parts.append(acc)
        hcv = jnp.concatenate(parts, axis=1)
        ms = _segment_mean(hcv * hcv, ones2_ref)
        y = hcv * lax.rsqrt(ms + EPS) * cg_ref[...]
        out_ref[pl.ds(r0, CONV_ROWS), :] = (y * jax.nn.sigmoid(y)).astype(BF16)
        return carry

    lax.fori_loop(0, ts // CONV_ROWS, chunk, 0, unroll=CONV_UNROLL)
    ext_ref[:, 0:CONV_HALO, :] = ext_ref[:, ts:ts + CONV_HALO, :]


def _conv(hc, cw, cb, cg, ones2, batch, seq):
    ts = TS_CONV
    nseq = seq // ts
    row = lambda b, i: (b * nseq + i, 0)
    return pl.pallas_call(
        _conv_kernel,
        grid=(batch, nseq),
        in_specs=[
            pl.BlockSpec((ts, CONV_CH), row),
            _const_spec((CONV_WIDTH, CONV_CH)),
            _const_spec((1, CONV_CH)),
            _const_spec((1, CONV_CH)),
            _const_spec((2 * LANES, LANES)),
        ],
        out_specs=pl.BlockSpec((ts, CONV_CH), row),
        out_shape=jax.ShapeDtypeStruct((batch * seq, CONV_CH), BF16),
        scratch_shapes=[pltpu.VMEM((CONV_CH // LANES, CONV_HALO + ts, LANES), F32)],
        compiler_params=_params("arbitrary", "arbitrary"),
        name="conv",
    )(hc, cw, cb, cg, ones2)


def _ret_kernel(rq_ref, rk_ref, rv_ref, rg_ref, decay_ref, qdec_ref, kdec_ref, sdec_ref, bd_ref,
                retg_ref, ones2_ref, out_ref, state_ref):
    @pl.when(pl.program_id(1) == 0)
    def _():
        state_ref[...] = jnp.zeros(state_ref.shape, F32)

    C = RET_CHUNK
    qlane_head = lax.broadcasted_iota(jnp.int32, (C, RQK_PAD), 1) // RET_KDIM
    vlane_first = (lax.broadcasted_iota(jnp.int32, (C, LANES), 1) // RET_VDIM) == 0
    npair = RET_W // LANES

    for c in range(rq_ref.shape[0] // C):
        rows = slice(c * C, (c + 1) * C)
        q = rq_ref[rows, :]
        k = rk_ref[rows, :]
        v = rv_ref[rows, :]

        zq = jnp.zeros_like(q)
        qs = jnp.concatenate([jnp.where(qlane_head == h, q, zq) for h in range(RET_HEADS)], axis=0)
        sc = lax.dot_general(qs, k, (((1,), (1,)), ((), ())), preferred_element_type=F32)
        p = (sc * decay_ref[...]).astype(BF16)

        inner = []
        for pr in range(npair):
            vp = v[:, pr * LANES:(pr + 1) * LANES]
            zv = jnp.zeros_like(vp)
            pcat = jnp.concatenate([p[(2 * pr) * C:(2 * pr + 1) * C], p[(2 * pr + 1) * C:(2 * pr + 2) * C]], axis=1)
            vst = jnp.concatenate([jnp.where(vlane_first, vp, zv), jnp.where(vlane_first, zv, vp)], axis=0)
            inner.append(jnp.dot(pcat, vst, preferred_element_type=F32))
        inner = jnp.concatenate(inner, axis=1)

        state = state_ref[...]
        cross = jnp.dot(q, state.astype(BF16), preferred_element_type=F32) * qdec_ref[...]
        vdec = (v.astype(F32) * kdec_ref[...]).astype(BF16)
        kv = lax.dot_general(k, vdec, (((0,), (0,)), ((), ())), preferred_element_type=F32)
        state_ref[...] = state * sdec_ref[...] + kv * bd_ref[...]

        o = inner + cross
        o = o * lax.rsqrt(_segment_mean(o * o, ones2_ref) + EPS) * retg_ref[...]
        out_ref[rows, :] = (o * rg_ref[rows, :].astype(F32)).astype(BF16)


def _retention(rq, rk, rv, rg, tables, retg, ones2, batch, seq):
    tr = TR_RET
    nseq = seq // tr
    row = lambda b, i: (b * nseq + i, 0)
    decay, qdec, kdec, sdec, bd = tables
    return pl.pallas_call(
        _ret_kernel,
        grid=(batch, nseq),
        in_specs=[
            pl.BlockSpec((tr, RQK_PAD), row),
            pl.BlockSpec((tr, RQK_PAD), row),
            pl.BlockSpec((tr, RET_W), row),
            pl.BlockSpec((tr, RET_W), row),
            _const_spec(decay.shape),
            _const_spec(qdec.shape),
            _const_spec(kdec.shape),
            _const_spec(sdec.shape),
            _const_spec(bd.shape),
            _const_spec((1, RET_W)),
            _const_spec((2 * LANES, LANES)),
        ],
        out_specs=pl.BlockSpec((tr, RET_W), row),
        out_shape=jax.ShapeDtypeStruct((batch * seq, RET_W), BF16),
        scratch_shapes=[pltpu.VMEM((RQK_PAD, RET_W), F32)],
        compiler_params=_params("arbitrary", "arbitrary"),
        name="retention",
    )(rq, rk, rv, rg, decay, qdec, kdec, sdec, bd, retg, ones2)


def _attn_kernel(q_ref, k_ref, v_ref, tbl_ref, ones_ref, out_ref,
                 kb1, vb1, kb4, vb4, kb16, vb16,
                 acc1, m1, s1, acc4, m4, s4, acc16, m16, s16):
    span_i = pl.program_id(2)
    L = ATT_BLOCK
    branches = ((1, kb1, vb1, acc1, m1, s1), (4, kb4, vb4, acc4, m4, s4), (16, kb16, vb16, acc16, m16, s16))

    @pl.when(span_i == 0)
    def _():
        for d, kb, vb, _, _, _ in branches:
            kb[0:L * d, :] = jnp.zeros((L * d, LANES), BF16)
            vb[0:L * d, :] = jnp.zeros((L * d, LANES), BF16)

    first_head = lax.broadcasted_iota(jnp.int32, (L, LANES), 1) < ATT_HDIM
    first_head2 = lax.broadcasted_iota(jnp.int32, (2 * L, LANES), 1) < ATT_HDIM

    for b, (d, kb, vb, acc_r, m_r, s_r) in enumerate(branches):
        prevlen = L * d
        shift = int(math.log2(d))

        def unit(u, carry, b=b, d=d, kb=kb, vb=vb, acc_r=acc_r, m_r=m_r, s_r=s_r, prevlen=prevlen, shift=shift):
            sub = lax.shift_right_logical(u, shift)
            res = u - lax.shift_left(sub, shift)
            off = pl.multiple_of(u * L, L)
            if d == 1:
                rows = pl.ds(off, L)
            else:
                rows = pl.ds(sub * (L * d) + res, L, stride=d)
            q = q_ref[rows, :].astype(BF16)
            kc = k_ref[rows, :].astype(BF16)
            vc = v_ref[rows, :].astype(BF16)
            kp = kb[pl.ds(off, L), :]
            vp = vb[pl.ds(off, L), :]
            kb[pl.ds(off + prevlen, L), :] = kc
            vb[pl.ds(off + prevlen, L), :] = vc
            kcat = jnp.concatenate([kp, kc], axis=0)
            vcat = jnp.concatenate([vp, vc], axis=0)
            zq = jnp.zeros_like(q)
            q2 = jnp.concatenate([jnp.where(first_head, q, zq), jnp.where(first_head, zq, q)], axis=0)
            variant = jnp.logical_and(span_i == 0, sub == 0).astype(jnp.int32)
            logits = lax.dot_general(q2, kcat, (((1,), (1,)), ((), ())), preferred_element_type=F32)
            logits = logits + tbl_ref[b, variant]
            m = jnp.max(logits, axis=-1, keepdims=True)
            pb = jnp.exp2(logits - m).astype(BF16)
            pcat = jnp.concatenate([pb[:L], pb[L:]], axis=1)
            zv = jnp.zeros_like(vcat)
            vst = jnp.concatenate([jnp.where(first_head2, vcat, zv), jnp.where(first_head2, zv, vcat)], axis=0)
            acc = jnp.dot(pcat, jnp.concatenate([vst, ones_ref[...]], axis=1), preferred_element_type=F32)
            acc_r[rows, :] = acc[:, :LANES]
            s_r[rows, :] = acc[:, LANES:]
            m_r[rows, :] = jnp.where(first_head, m[:L], m[L:])
            return carry

        lax.fori_loop(0, SPAN // L, unit, 0, unroll=UNIT_UNROLL)

    CR = 256

    def combine(c, carry):
        rows = pl.ds(pl.multiple_of(c * CR, CR), CR)
        ma, mb, mc = m1[rows, :], m4[rows, :], m16[rows, :]
        mx = jnp.maximum(jnp.maximum(ma, mb), mc)
        ea, eb, ec = jnp.exp2(ma - mx), jnp.exp2(mb - mx), jnp.exp2(mc - mx)
        num = ea * acc1[rows, :] + eb * acc4[rows, :] + ec * acc16[rows, :]
        den = ea * s1[rows, :] + eb * s4[rows, :] + ec * s16[rows, :]
        out_ref[rows, :] = (num / den).astype(BF16)
        return carry

    lax.fori_loop(0, SPAN // CR, combine, 0)

    for d, kb, vb, _, _, _ in branches:
        kb[0:L * d, :] = kb[SPAN:SPAN + L * d, :]
        vb[0:L * d, :] = vb[SPAN:SPAN + L * d, :]


def _rowsum_ones():
    row_first = jnp.arange(4 * ATT_BLOCK)[:, None] < 2 * ATT_BLOCK
    lane_first = jnp.arange(LANES)[None, :] < ATT_HDIM
    return (row_first == lane_first).astype(BF16)


def _attention(aq, ak, av, tbl, batch, seq):
    nslab = ATT_W // LANES
    nspan = seq // SPAN
    blk = lambda b, p, i: (p, b * nspan + i, 0)
    scratch = []
    for d in DILATIONS:
        scratch += [pltpu.VMEM((ATT_BLOCK * d + SPAN, LANES), BF16)] * 2
    scratch += [pltpu.VMEM((SPAN, LANES), F32)] * 9
    return pl.pallas_call(
        _attn_kernel,
        grid=(batch, nslab, nspan),
        in_specs=[
            pl.BlockSpec((None, SPAN, LANES), blk),
            pl.BlockSpec((None, SPAN, LANES), blk),
            pl.BlockSpec((None, SPAN, LANES), blk),
            pl.BlockSpec((None, len(DILATIONS), 2, 2 * ATT_BLOCK, 2 * ATT_BLOCK), lambda b, p, i: (p, 0, 0, 0, 0)),
            _const_spec((4 * ATT_BLOCK, LANES)),
        ],
        out_specs=pl.BlockSpec((None, SPAN, LANES), blk),
        out_shape=jax.ShapeDtypeStruct((nslab, batch * seq, LANES), BF16),
        scratch_shapes=scratch,
        compiler_params=_params("arbitrary", "arbitrary", "arbitrary"),
        name="dilated_attn",
    )(aq, ak, av, tbl, _rowsum_ones())


def _out_ffn_kernel(x_ref, conv_ref, ret_ref, att_ref, wout_ref, g2_ref, w1_ref, w2_ref, o_ref):
    mix = jnp.concatenate([conv_ref[...], ret_ref[...]] + [att_ref[p] for p in range(ATT_W // LANES)], axis=1)
    x1 = x_ref[...] + jnp.dot(mix, wout_ref[...], preferred_element_type=F32)
    h2 = (x1 * lax.rsqrt(jnp.mean(x1 * x1, axis=-1, keepdims=True) + EPS) * g2_ref[...]).astype(BF16)
    acc = x1
    for c in range(D_FF // FF_CHUNK):
        sl = slice(c * FF_CHUNK, (c + 1) * FF_CHUNK)
        t = jnp.maximum(jnp.dot(h2, w1_ref[:, sl], preferred_element_type=F32), 0.0)
        acc = acc + jnp.dot((t * t).astype(BF16), w2_ref[sl, :], preferred_element_type=F32)
    o_ref[...] = acc


def _out_ffn(x2, conv_o, ret_o, att_o, w_out, g2, w1, w2):
    T = x2.shape[0]
    tm = TM_FFN
    row = lambda i: (i, 0)
    return pl.pallas_call(
        _out_ffn_kernel,
        grid=(T // tm,),
        in_specs=[
            pl.BlockSpec((tm, D_MODEL), row),
            pl.BlockSpec((tm, CONV_CH), row),
            pl.BlockSpec((tm, RET_W), row),
            pl.BlockSpec((ATT_W // LANES, tm, LANES), lambda i: (0, i, 0)),
            _const_spec((D_MODEL, D_MODEL)),
            _const_spec((1, D_MODEL)),
            _const_spec((D_MODEL, D_FF)),
            _const_spec((D_FF, D_MODEL)),
        ],
        out_specs=pl.BlockSpec((tm, D_MODEL), row),
        out_shape=jax.ShapeDtypeStruct((T, D_MODEL), F32),
        compiler_params=_params("arbitrary"),
        name="out_ffn",
    )(x2, conv_o, ret_o, att_o, w_out, g2, w1, w2)


def _rotary_tables(seq):
    half = RET_KDIM // 2
    inv = 1.0 / (ROPE_BASE ** jnp.linspace(0.0, 1.0, half, dtype=F32))
    ang = jnp.arange(seq, dtype=F32)[:, None] * inv[None, :]
    c, s = jnp.cos(ang), jnp.sin(ang)
    cos_h = jnp.repeat(c, 2, axis=1)
    sin_h = jnp.stack([-s, s], axis=-1).reshape(seq, RET_KDIM)
    reps = LANES // RET_KDIM
    return jnp.tile(cos_h, (1, reps)), jnp.tile(sin_h, (1, reps))


def _retention_tables():
    C = RET_CHUNK
    H = RET_HEADS
    scale = RET_KDIM ** -0.5
    log_gamma = jnp.log(1.0 - 2.0 ** (-5.0 - jnp.arange(H, dtype=F32)))
    idx = jnp.arange(C, dtype=F32)
    diff = idx[:, None] - idx[None, :]
    decay = jnp.where(diff >= 0, jnp.exp(log_gamma[:, None, None] * jnp.maximum(diff, 0.0)), 0.0)
    decay = (decay * scale).reshape(H * C, C)
    lane_head = jnp.arange(RET_W) // RET_VDIM
    qdec = jnp.exp(log_gamma[lane_head][None, :] * (idx[:, None] + 1.0))
    kdec = jnp.exp(log_gamma[lane_head][None, :] * (C - 1.0 - idx[:, None])) * scale
    row_head = jnp.arange(RQK_PAD) // RET_KDIM
    bd = (row_head[:, None] == lane_head[None, :]).astype(F32)
    sdec = bd * jnp.exp(log_gamma * C)[lane_head][None, :]
    return decay, qdec, kdec, sdec, bd


def _t5_bucket(dist):
    max_exact = N_BUCKETS // 2
    nf = jnp.maximum(dist, 1).astype(F32)
    large = max_exact + (jnp.log(nf / max_exact) / math.log(MAX_DISTANCE / max_exact)
                         * (N_BUCKETS - max_exact)).astype(jnp.int32)
    large = jnp.minimum(large, N_BUCKETS - 1)
    return jnp.where(dist < max_exact, dist, large)


def _attention_tables(rel_bias):
    Lb = ATT_BLOCK
    qi = jnp.arange(Lb, dtype=jnp.int32)[:, None]
    kj = jnp.arange(2 * Lb, dtype=jnp.int32)[None, :]
    dist = Lb + qi - kj
    band = (dist >= 0) & (dist <= Lb)
    per_branch = []
    for d in DILATIONS:
        bucket = _t5_bucket(jnp.clip(dist, 0, Lb) * d)
        rb = rel_bias.astype(F32)
        bias = jnp.zeros((ATT_HEADS, Lb, 2 * Lb), F32)
        for n in range(N_BUCKETS):
            bias = jnp.where((bucket == n)[None], rb[n][:, None, None], bias)
        bias = bias * LOG2_E
        full = jnp.where(band[None], bias, NEG_INF)
        first = jnp.where((band & (kj >= Lb))[None], bias, NEG_INF)
        per_branch.append(jnp.stack([full, first], axis=0))
    t = jnp.stack(per_branch, axis=0)
    nslab = ATT_W // LANES
    t = t.reshape(len(DILATIONS), 2, nslab, 2 * Lb, 2 * Lb)
    return t.transpose(2, 0, 1, 3, 4)


def _pad_w_in(w_in):
    splits = [sum(_IN_SIZES[:i + 1]) for i in range(len(_IN_SIZES) - 1)]
    parts = jnp.split(w_in, splits, axis=1)
    zpad = jnp.zeros((w_in.shape[0], RQK_PAD - RET_HEADS * RET_KDIM), w_in.dtype)
    return jnp.concatenate([parts[0], parts[1], zpad, parts[2], zpad] + parts[3:], axis=1)


def _stacked_block_ones():
    g = jnp.arange(LANES) // SEG
    blk = ((g[:, None] == g[None, :]).astype(F32) * (1.0 / SEG)).astype(BF16)
    return jnp.concatenate([blk, blk], axis=0)


def kernel(x, norm1_g, w_in, conv_w, conv_b, conv_g, ret_g, q_g, k_g, w_out, norm2_g, w_ff1, w_ff2, rel_bias):
    batch, seq, _ = x.shape
    depth = w_in.shape[0]
    assert seq % SPAN == 0 and seq % TS_CONV == 0 and seq % TR_RET == 0 and seq % TM_PROJ == 0
    assert (batch * seq) % TM_FFN == 0

    cos_t, sin_t = _rotary_tables(seq)
    ret_tables = _retention_tables()
    att_tbl = _attention_tables(rel_bias)
    ones2 = _stacked_block_ones()

    x2 = x.reshape(batch * seq, D_MODEL)
    for l in range(depth):
        w_in_l = _pad_w_in(w_in[l]).astype(BF16)
        qg = jnp.tile(q_g[l].astype(F32), ATT_HEADS)[None, :] * (ATT_HDIM ** -0.5 * LOG2_E)
        kg = jnp.tile(k_g[l].astype(F32), ATT_HEADS)[None, :]
        hc, rq, rk, rv, rg, aq, ak, av = _in_proj(
            x2, norm1_g[l].astype(F32)[None, :], w_in_l, cos_t, sin_t, qg, kg, ones2, seq)
        conv_o = _conv(hc, conv_w[l].astype(F32), conv_b[l].astype(F32)[None, :],
                       conv_g[l].astype(F32)[None, :], ones2, batch, seq)
        ret_o = _retention(rq, rk, rv, rg, ret_tables, ret_g[l].astype(F32)[None, :], ones2, batch, seq)
        att_o = _attention(aq, ak, av, att_tbl, batch, seq)
        x2 = _out_ffn(x2, conv_o, ret_o, att_o, w_out[l].astype(BF16), norm2_g[l].astype(F32)[None, :],
                      w_ff1[l].astype(BF16), w_ff2[l].astype(BF16))
    return x2.reshape(batch, seq, D_MODEL)
```

```python
import functools
import math

import jax
import jax.numpy as jnp
from jax import lax
from jax.experimental import pallas as pl
from jax.experimental.pallas import tpu as pltpu

F32 = jnp.float32
BF16 = jnp.bfloat16

D_MODEL = 1024
CONV_CH = 256
SEG = 64
CONV_WIDTH = 31
RET_HEADS = 6
RET_KDIM = 32
RET_VDIM = 64
RET_W = RET_HEADS * RET_VDIM
ATT_HEADS = 6
ATT_HDIM = 64
ATT_W = ATT_HEADS * ATT_HDIM
DILATIONS = (1, 4, 16)
ATT_BLOCK = 128
N_BUCKETS = 32
MAX_DISTANCE = 2048
D_FF = 4 * D_MODEL
EPS = 1e-6
ROPE_BASE = 10000.0
NEG_INF = -1e30
LOG2_E = math.log2(math.e)

LANES = 128
RQK_PAD = 256
VMEM_LIMIT_BYTES = 56 * 1024 * 1024

_IN_SIZES = (2 * CONV_CH, RET_HEADS * RET_KDIM, RET_HEADS * RET_KDIM, RET_W, RET_W, ATT_W, ATT_W, ATT_W)
C_CONV = 0
C_RQ = 512
C_RK = C_RQ + RQK_PAD
C_RV = C_RK + RQK_PAD
C_RG = C_RV + RET_W
C_AQ = C_RG + RET_W
C_AK = C_AQ + ATT_W
C_AV = C_AK + ATT_W
IN_W_PAD = C_AV + ATT_W

TM_PROJ = 512
TM_FFN = 512
CONV_ROWS = 64
CONV_HALO = 32
TR_RET = 512
RET_CHUNK = 128
SPAN = ATT_BLOCK * DILATIONS[-1]
FF_CHUNK = 1024
UNIT_UNROLL = 16


def _const_spec(shape):
    nd = len(shape)
    return pl.BlockSpec(shape, lambda *_: (0,) * nd, pipeline_mode=pl.Buffered(1))


def _params(*sem):
    return pltpu.CompilerParams(dimension_semantics=sem, vmem_limit_bytes=VMEM_LIMIT_BYTES)


def _segment_mean(y, ones2_ref):
    ones2 = ones2_ref[...]
    parts = []
    for c in range(y.shape[1] // LANES):
        t = y[:, c * LANES:(c + 1) * LANES]
        hi = t.astype(BF16)
        lo = (t - hi.astype(F32)).astype(BF16)
        parts.append(jnp.dot(jnp.concatenate([hi, lo], axis=1), ones2, preferred_element_type=F32))
    return parts[0] if len(parts) == 1 else jnp.concatenate(parts, axis=1)


def _in_proj_kernel(x_ref, g1_ref, w_ref, cos_ref, sin_ref, qg_ref, kg_ref, ones2_ref,
                    hc_ref, rq_ref, rk_ref, rv_ref, rg_ref, aq_ref, ak_ref, av_ref):
    x = x_ref[...]
    h = x * lax.rsqrt(jnp.mean(x * x, axis=-1, keepdims=True) + EPS) * g1_ref[...]
    hb = h.astype(BF16)

    def proj(lo, hi):
        return jnp.dot(hb, w_ref[:, lo:hi], preferred_element_type=F32)

    u = proj(C_CONV, C_RV)
    hc_ref[...] = u[:, :CONV_CH] * jax.nn.sigmoid(u[:, CONV_CH:2 * CONV_CH])

    even = (lax.broadcasted_iota(jnp.int32, (1, LANES), 1) % 2) == 0
    cos = cos_ref[...]
    sin = sin_ref[...]
    for base, dst in ((C_RQ, rq_ref), (C_RK, rk_ref)):
        for c in range(RQK_PAD // LANES):
            t = u[:, base + c * LANES: base + (c + 1) * LANES]
            sw = jnp.where(even, pltpu.roll(t, LANES - 1, 1), pltpu.roll(t, 1, 1))
            dst[:, c * LANES:(c + 1) * LANES] = (t * cos + sw * sin).astype(BF16)

    r = proj(C_RV, C_AQ)
    rv_ref[...] = r[:, :RET_W].astype(BF16)
    g = r[:, RET_W:]
    rg_ref[...] = (g * jax.nn.sigmoid(g)).astype(BF16)

    a = proj(C_AQ, IN_W_PAD)
    aq = a[:, :ATT_W]
    aq = aq * lax.rsqrt(_segment_mean(aq * aq, ones2_ref) + EPS) * qg_ref[...]
    ak = a[:, ATT_W:2 * ATT_W]
    ak = ak * lax.rsqrt(_segment_mean(ak * ak, ones2_ref) + EPS) * kg_ref[...]
    av = a[:, 2 * ATT_W:]
    for p in range(ATT_W // LANES):
        sl = slice(p * LANES, (p + 1) * LANES)
        aq_ref[p] = aq[:, sl]
        ak_ref[p] = ak[:, sl]
        av_ref[p] = av[:, sl]


def _in_proj(x2, g1, w_in, cos_t, sin_t, qg, kg, ones2, seq):
    T = x2.shape[0]
    tm = TM_PROJ
    nseq = seq // tm
    row = lambda i: (i, 0)
    pos = lambda i: (i % nseq, 0)
    slab = lambda i: (0, i, 0)
    return pl.pallas_call(
        _in_proj_kernel,
        grid=(T // tm,),
        in_specs=[
            pl.BlockSpec((tm, D_MODEL), row),
            _const_spec((1, D_MODEL)),
            _const_spec((D_MODEL, IN_W_PAD)),
            pl.BlockSpec((tm, LANES), pos),
            pl.BlockSpec((tm, LANES), pos),
            _const_spec((1, ATT_W)),
            _const_spec((1, ATT_W)),
            _const_spec((2 * LANES, LANES)),
        ],
        out_specs=[
            pl.BlockSpec((tm, CONV_CH), row),
            pl.BlockSpec((tm, RQK_PAD), row),
            pl.BlockSpec((tm, RQK_PAD), row),
            pl.BlockSpec((tm, RET_W), row),
            pl.BlockSpec((tm, RET_W), row),
            pl.BlockSpec((ATT_W // LANES, tm, LANES), slab),
            pl.BlockSpec((ATT_W // LANES, tm, LANES), slab),
            pl.BlockSpec((ATT_W // LANES, tm, LANES), slab),
        ],
        out_shape=[
            jax.ShapeDtypeStruct((T, CONV_CH), F32),
            jax.ShapeDtypeStruct((T, RQK_PAD), BF16),
            jax.ShapeDtypeStruct((T, RQK_PAD), BF16),
            jax.ShapeDtypeStruct((T, RET_W), BF16),
            jax.ShapeDtypeStruct((T, RET_W), BF16),
            jax.ShapeDtypeStruct((ATT_W // LANES, T, LANES), F32),
            jax.ShapeDtypeStruct((ATT_W // LANES, T, LANES), F32),
            jax.ShapeDtypeStruct((ATT_W // LANES, T, LANES), F32),
        ],
        compiler_params=_params("arbitrary"),
        name="in_proj",
    )(x2, g1, w_in, cos_t, sin_t, qg, kg, ones2)


def _conv_tile(hc, ext_ref, cw_ref, cb_ref, cg_ref, ones2_ref):
    tm = hc.shape[0]
    nslab = CONV_CH // LANES
    for j in range(nslab):
        ext_ref[j, CONV_HALO:CONV_HALO + tm, :] = hc[:, j * LANES:(j + 1) * LANES]
    first_tap = CONV_HALO - (CONV_WIDTH - 1)
    chunks = []
    for c in range(tm // CONV_ROWS):
        parts = []
        for j in range(nslab):
            sl = slice(j * LANES, (j + 1) * LANES)
            acc = jnp.broadcast_to(cb_ref[:, sl], (CONV_ROWS, LANES))
            for w in range(CONV_WIDTH):
                r0 = c * CONV_ROWS + first_tap + w
                acc = acc + ext_ref[j, r0:r0 + CONV_ROWS, :] * cw_ref[w:w + 1, sl]
            parts.append(acc)
        chunks.append(jnp.concatenate(parts, axis=1))
    hcv = jnp.concatenate(chunks, axis=0)
    y = hcv * lax.rsqrt(_segment_mean(hcv * hcv, ones2_ref) + EPS) * cg_ref[...]
    ext_ref[:, 0:CONV_HALO, :] = ext_ref[:, tm:tm + CONV_HALO, :]
    return (y * jax.nn.sigmoid(y)).astype(BF16)


def _ret_kernel(rq_ref, rk_ref, rv_ref, rg_ref, decay_ref, qdec_ref, kdec_ref, sdec_ref, bd_ref,
                retg_ref, ones2_ref, out_ref, state_ref):
    @pl.when(pl.program_id(1) == 0)
    def _():
        state_ref[...] = jnp.zeros(state_ref.shape, F32)

    C = RET_CHUNK
    qlane_head = lax.broadcasted_iota(jnp.int32, (C, RQK_PAD), 1) // RET_KDIM
    vlane_first = (lax.broadcasted_iota(jnp.int32, (C, LANES), 1) // RET_VDIM) == 0
    npair = RET_W // LANES

    for c in range(rq_ref.shape[0] // C):
        rows = slice(c * C, (c + 1) * C)
        q = rq_ref[rows, :]
        k = rk_ref[rows, :]
        v = rv_ref[rows, :]

        zq = jnp.zeros_like(q)
        qs = jnp.concatenate([jnp.where(qlane_head == h, q, zq) for h in range(RET_HEADS)], axis=0)
        sc = lax.dot_general(qs, k, (((1,), (1,)), ((), ())), preferred_element_type=F32)
        p = (sc * decay_ref[...]).astype(BF16)

        inner = []
        for pr in range(npair):
            vp = v[:, pr * LANES:(pr + 1) * LANES]
            zv = jnp.zeros_like(vp)
            pcat = jnp.concatenate([p[(2 * pr) * C:(2 * pr + 1) * C], p[(2 * pr + 1) * C:(2 * pr + 2) * C]], axis=1)
            vst = jnp.concatenate([jnp.where(vlane_first, vp, zv), jnp.where(vlane_first, zv, vp)], axis=0)
            inner.append(jnp.dot(pcat, vst, preferred_element_type=F32))
        inner = jnp.concatenate(inner, axis=1)

        state = state_ref[...]
        cross = jnp.dot(q, state.astype(BF16), preferred_element_type=F32) * qdec_ref[...]
        vdec = (v.astype(F32) * kdec_ref[...]).astype(BF16)
        kv = lax.dot_general(k, vdec, (((0,), (0,)), ((), ())), preferred_element_type=F32)
        state_ref[...] = state * sdec_ref[...] + kv * bd_ref[...]

        o = inner + cross
        o = o * lax.rsqrt(_segment_mean(o * o, ones2_ref) + EPS) * retg_ref[...]
        out_ref[rows, :] = (o * rg_ref[rows, :].astype(F32)).astype(BF16)


def _retention(rq, rk, rv, rg, tables, retg, ones2, batch, seq):
    tr = TR_RET
    nseq = seq // tr
    row = lambda b, i: (b * nseq + i, 0)
    decay, qdec, kdec, sdec, bd = tables
    return pl.pallas_call(
        _ret_kernel,
        grid=(batch, nseq),
        in_specs=[
            pl.BlockSpec((tr, RQK_PAD), row),
            pl.BlockSpec((tr, RQK_PAD), row),
            pl.BlockSpec((tr, RET_W), row),
            pl.BlockSpec((tr, RET_W), row),
            _const_spec(decay.shape),
            _const_spec(qdec.shape),
            _const_spec(kdec.shape),
            _const_spec(sdec.shape),
            _const_spec(bd.shape),
            _const_spec((1, RET_W)),
            _const_spec((2 * LANES, LANES)),
        ],
        out_specs=pl.BlockSpec((tr, RET_W), row),
        out_shape=jax.ShapeDtypeStruct((batch * seq, RET_W), BF16),
        scratch_shapes=[pltpu.VMEM((RQK_PAD, RET_W), F32)],
        compiler_params=_params("arbitrary", "arbitrary"),
        name="retention",
    )(rq, rk, rv, rg, decay, qdec, kdec, sdec, bd, retg, ones2)


def _attn_kernel(q_ref, k_ref, v_ref, tbl_ref, ones_ref, out_ref,
                 kb1, vb1, kb4, vb4, kb16, vb16,
                 acc1, m1, s1, acc4, m4, s4, acc16, m16, s16):
    span_i = pl.program_id(2)
    L = ATT_BLOCK
    branches = ((1, kb1, vb1, acc1, m1, s1), (4, kb4, vb4, acc4, m4, s4), (16, kb16, vb16, acc16, m16, s16))

    @pl.when(span_i == 0)
    def _():
        for d, kb, vb, _, _, _ in branches:
            kb[0:L * d, :] = jnp.zeros((L * d, LANES), BF16)
            vb[0:L * d, :] = jnp.zeros((L * d, LANES), BF16)

    first_head = lax.broadcasted_iota(jnp.int32, (L, LANES), 1) < ATT_HDIM
    first_head2 = lax.broadcasted_iota(jnp.int32, (2 * L, LANES), 1) < ATT_HDIM

    for b, (d, kb, vb, acc_r, m_r, s_r) in enumerate(branches):
        prevlen = L * d
        shift = int(math.log2(d))

        def unit(u, carry, b=b, d=d, kb=kb, vb=vb, acc_r=acc_r, m_r=m_r, s_r=s_r, prevlen=prevlen, shift=shift):
            sub = lax.shift_right_logical(u, shift)
            res = u - lax.shift_left(sub, shift)
            off = pl.multiple_of(u * L, L)
            if d == 1:
                rows = pl.ds(off, L)
            else:
                rows = pl.ds(sub * (L * d) + res, L, stride=d)
            q = q_ref[rows, :].astype(BF16)
            kc = k_ref[rows, :].astype(BF16)
            vc = v_ref[rows, :].astype(BF16)
            kp = kb[pl.ds(off, L), :]
            vp = vb[pl.ds(off, L), :]
            kb[pl.ds(off + prevlen, L), :] = kc
            vb[pl.ds(off + prevlen, L), :] = vc
            kcat = jnp.concatenate([kp, kc], axis=0)
            vcat = jnp.concatenate([vp, vc], axis=0)
            zq = jnp.zeros_like(q)
            q2 = jnp.concatenate([jnp.where(first_head, q, zq), jnp.where(first_head, zq, q)], axis=0)
            variant = jnp.logical_and(span_i == 0, sub == 0).astype(jnp.int32)
            logits = lax.dot_general(q2, kcat, (((1,), (1,)), ((), ())), preferred_element_type=F32)
            logits = logits + tbl_ref[b, variant]
            m = jnp.max(logits, axis=-1, keepdims=True)
            pb = jnp.exp2(logits - m).astype(BF16)
            pcat = jnp.concatenate([pb[:L], pb[L:]], axis=1)
            zv = jnp.zeros_like(vcat)
            vst = jnp.concatenate([jnp.where(first_head2, vcat, zv), jnp.where(first_head2, zv, vcat)], axis=0)
            acc = jnp.dot(pcat, jnp.concatenate([vst, ones_ref[...]], axis=1), preferred_element_type=F32)
            acc_r[rows, :] = acc[:, :LANES]
            s_r[rows, :] = acc[:, LANES:]
            m_r[rows, :] = jnp.where(first_head, m[:L], m[L:])
            return carry

        lax.fori_loop(0, SPAN // L, unit, 0, unroll=UNIT_UNROLL)

    CR = 256

    def combine(c, carry):
        rows = pl.ds(pl.multiple_of(c * CR, CR), CR)
        ma, mb, mc = m1[rows, :], m4[rows, :], m16[rows, :]
        mx = jnp.maximum(jnp.maximum(ma, mb), mc)
        ea, eb, ec = jnp.exp2(ma - mx), jnp.exp2(mb - mx), jnp.exp2(mc - mx)
        num = ea * acc1[rows, :] + eb * acc4[rows, :] + ec * acc16[rows, :]
        den = ea * s1[rows, :] + eb * s4[rows, :] + ec * s16[rows, :]
        out_ref[rows, :] = (num / den).astype(BF16)
        return carry

    lax.fori_loop(0, SPAN // CR, combine, 0)

    for d, kb, vb, _, _, _ in branches:
        kb[0:L * d, :] = kb[SPAN:SPAN + L * d, :]
        vb[0:L * d, :] = vb[SPAN:SPAN + L * d, :]


def _rowsum_ones():
    row_first = jnp.arange(4 * ATT_BLOCK)[:, None] < 2 * ATT_BLOCK
    lane_first = jnp.arange(LANES)[None, :] < ATT_HDIM
    return (row_first == lane_first).astype(BF16)


def _attention(aq, ak, av, tbl, batch, seq):
    nslab = ATT_W // LANES
    nspan = seq // SPAN
    blk = lambda b, p, i: (p, b * nspan + i, 0)
    scratch = []
    for d in DILATIONS:
        scratch += [pltpu.VMEM((ATT_BLOCK * d + SPAN, LANES), BF16)] * 2
    scratch += [pltpu.VMEM((SPAN, LANES), F32)] * 9
    return pl.pallas_call(
        _attn_kernel,
        grid=(batch, nslab, nspan),
        in_specs=[
            pl.BlockSpec((None, SPAN, LANES), blk),
            pl.BlockSpec((None, SPAN, LANES), blk),
            pl.BlockSpec((None, SPAN, LANES), blk),
            pl.BlockSpec((None, len(DILATIONS), 2, 2 * ATT_BLOCK, 2 * ATT_BLOCK), lambda b, p, i: (p, 0, 0, 0, 0)),
            _const_spec((4 * ATT_BLOCK, LANES)),
        ],
        out_specs=pl.BlockSpec((None, SPAN, LANES), blk),
        out_shape=jax.ShapeDtypeStruct((nslab, batch * seq, LANES), BF16),
        scratch_shapes=scratch,
        compiler_params=_params("arbitrary", "arbitrary", "arbitrary"),
        name="dilated_attn",
    )(aq, ak, av, tbl, _rowsum_ones())


def _out_ffn_kernel(x_ref, hc0_ref, hcn_ref, ret_ref, att_ref, wout_ref, g2_ref, w1_ref, w2_ref,
                    cw_ref, cb_ref, cg_ref, ones2_ref, o_ref, ext_ref, convbuf_ref, *, tiles_per_seq):
    i = pl.program_id(0)
    nslab = CONV_CH // LANES
    conv_args = (ext_ref, cw_ref, cb_ref, cg_ref, ones2_ref)

    @pl.when(i == 0)
    def _():
        ext_ref[:, 0:CONV_HALO, :] = jnp.zeros((nslab, CONV_HALO, LANES), F32)
        convbuf_ref[0] = _conv_tile(hc0_ref[...], *conv_args)

    slot = lax.rem(i, 2)
    next_starts_seq = lax.rem(i + 1, tiles_per_seq) == 0
    ext_ref[:, 0:CONV_HALO, :] = jnp.where(next_starts_seq, 0.0, ext_ref[:, 0:CONV_HALO, :])
    convbuf_ref[1 - slot] = _conv_tile(hcn_ref[...], *conv_args)

    mix = jnp.concatenate([convbuf_ref[slot], ret_ref[...]] + [att_ref[p] for p in range(ATT_W // LANES)], axis=1)
    x1 = x_ref[...] + jnp.dot(mix, wout_ref[...], preferred_element_type=F32)
    h2 = (x1 * lax.rsqrt(jnp.mean(x1 * x1, axis=-1, keepdims=True) + EPS) * g2_ref[...]).astype(BF16)
    acc = x1
    for c in range(D_FF // FF_CHUNK):
        sl = slice(c * FF_CHUNK, (c + 1) * FF_CHUNK)
        t = jnp.maximum(jnp.dot(h2, w1_ref[:, sl], preferred_element_type=F32), 0.0)
        acc = acc + jnp.dot((t * t).astype(BF16), w2_ref[sl, :], preferred_element_type=F32)
    o_ref[...] = acc


def _out_ffn(x2, hc, ret_o, att_o, w_out, g2, w1, w2, cw, cb, cg, ones2, seq):
    T = x2.shape[0]
    tm = TM_FFN
    ntile = T // tm
    row = lambda i: (i, 0)
    return pl.pallas_call(
        functools.partial(_out_ffn_kernel, tiles_per_seq=seq // tm),
        grid=(ntile,),
        in_specs=[
            pl.BlockSpec((tm, D_MODEL), row),
            _const_spec((tm, CONV_CH)),
            pl.BlockSpec((tm, CONV_CH), lambda i: (jnp.minimum(i + 1, ntile - 1), 0)),
            pl.BlockSpec((tm, RET_W), row),
            pl.BlockSpec((ATT_W // LANES, tm, LANES), lambda i: (0, i, 0)),
            _const_spec((D_MODEL, D_MODEL)),
            _const_spec((1, D_MODEL)),
            _const_spec((D_MODEL, D_FF)),
            _const_spec((D_FF, D_MODEL)),
            _const_spec((CONV_WIDTH, CONV_CH)),
            _const_spec((1, CONV_CH)),
            _const_spec((1, CONV_CH)),
            _const_spec((2 * LANES, LANES)),
        ],
        out_specs=pl.BlockSpec((tm, D_MODEL), row),
        out_shape=jax.ShapeDtypeStruct((T, D_MODEL), F32),
        scratch_shapes=[pltpu.VMEM((CONV_CH // LANES, CONV_HALO + tm, LANES), F32),
                        pltpu.VMEM((2, tm, CONV_CH), BF16)],
        compiler_params=_params("arbitrary"),
        name="out_ffn",
    )(x2, hc, hc, ret_o, att_o, w_out, g2, w1, w2, cw, cb, cg, ones2)


def _rotary_tables(seq):
    half = RET_KDIM // 2
    inv = 1.0 / (ROPE_BASE ** jnp.linspace(0.0, 1.0, half, dtype=F32))
    ang = jnp.arange(seq, dtype=F32)[:, None] * inv[None, :]
    c, s = jnp.cos(ang), jnp.sin(ang)
    cos_h = jnp.repeat(c, 2, axis=1)
    sin_h = jnp.stack([-s, s], axis=-1).reshape(seq, RET_KDIM)
    reps = LANES // RET_KDIM
    return jnp.tile(cos_h, (1, reps)), jnp.tile(sin_h, (1, reps))


def _retention_tables():
    C = RET_CHUNK
    H = RET_HEADS
    scale = RET_KDIM ** -0.5
    log_gamma = jnp.log(1.0 - 2.0 ** (-5.0 - jnp.arange(H, dtype=F32)))
    idx = jnp.arange(C, dtype=F32)
    diff = idx[:, None] - idx[None, :]
    decay = jnp.where(diff >= 0, jnp.exp(log_gamma[:, None, None] * jnp.maximum(diff, 0.0)), 0.0)
    decay = (decay * scale).reshape(H * C, C)
    lane_head = jnp.arange(RET_W) // RET_VDIM
    qdec = jnp.exp(log_gamma[lane_head][None, :] * (idx[:, None] + 1.0))
    kdec = jnp.exp(log_gamma[lane_head][None, :] * (C - 1.0 - idx[:, None])) * scale
    row_head = jnp.arange(RQK_PAD) // RET_KDIM
    bd = (row_head[:, None] == lane_head[None, :]).astype(F32)
    sdec = bd * jnp.exp(log_gamma * C)[lane_head][None, :]
    return decay, qdec, kdec, sdec, bd


def _t5_bucket(dist):
    max_exact = N_BUCKETS // 2
    nf = jnp.maximum(dist, 1).astype(F32)
    large = max_exact + (jnp.log(nf / max_exact) / math.log(MAX_DISTANCE / max_exact)
                         * (N_BUCKETS - max_exact)).astype(jnp.int32)
    large = jnp.minimum(large, N_BUCKETS - 1)
    return jnp.where(dist < max_exact, dist, large)


def _attention_tables(rel_bias):
    Lb = ATT_BLOCK
    qi = jnp.arange(Lb, dtype=jnp.int32)[:, None]
    kj = jnp.arange(2 * Lb, dtype=jnp.int32)[None, :]
    dist = Lb + qi - kj
    band = (dist >= 0) & (dist <= Lb)
    per_branch = []
    for d in DILATIONS:
        bucket = _t5_bucket(jnp.clip(dist, 0, Lb) * d)
        rb = rel_bias.astype(F32)
        bias = jnp.zeros((ATT_HEADS, Lb, 2 * Lb), F32)
        for n in range(N_BUCKETS):
            bias = jnp.where((bucket == n)[None], rb[n][:, None, None], bias)
        bias = bias * LOG2_E
        full = jnp.where(band[None], bias, NEG_INF)
        first = jnp.where((band & (kj >= Lb))[None], bias, NEG_INF)
        per_branch.append(jnp.stack([full, first], axis=0))
    t = jnp.stack(per_branch, axis=0)
    nslab = ATT_W // LANES
    t = t.reshape(len(DILATIONS), 2, nslab, 2 * Lb, 2 * Lb)
    return t.transpose(2, 0, 1, 3, 4)


def _pad_w_in(w_in):
    splits = [sum(_IN_SIZES[:i + 1]) for i in range(len(_IN_SIZES) - 1)]
    parts = jnp.split(w_in, splits, axis=1)
    zpad = jnp.zeros((w_in.shape[0], RQK_PAD - RET_HEADS * RET_KDIM), w_in.dtype)
    return jnp.concatenate([parts[0], parts[1], zpad, parts[2], zpad] + parts[3:], axis=1)


def _stacked_block_ones():
    g = jnp.arange(LANES) // SEG
    blk = ((g[:, None] == g[None, :]).astype(F32) * (1.0 / SEG)).astype(BF16)
    return jnp.concatenate([blk, blk], axis=0)


def kernel(x, norm1_g, w_in, conv_w, conv_b, conv_g, ret_g, q_g, k_g, w_out, norm2_g, w_ff1, w_ff2, rel_bias):
    batch, seq, _ = x.shape
    depth = w_in.shape[0]
    assert seq % SPAN == 0 and seq % TR_RET == 0 and seq % TM_PROJ == 0 and seq % TM_FFN == 0

    cos_t, sin_t = _rotary_tables(seq)
    ret_tables = _retention_tables()
    att_tbl = _attention_tables(rel_bias)
    ones2 = _stacked_block_ones()

    x2 = x.reshape(batch * seq, D_MODEL)
    for l in range(depth):
        w_in_l = _pad_w_in(w_in[l]).astype(BF16)
        qg = jnp.tile(q_g[l].astype(F32), ATT_HEADS)[None, :] * (ATT_HDIM ** -0.5 * LOG2_E)
        kg = jnp.tile(k_g[l].astype(F32), ATT_HEADS)[None, :]
        hc, rq, rk, rv, rg, aq, ak, av = _in_proj(
            x2, norm1_g[l].astype(F32)[None, :], w_in_l, cos_t, sin_t, qg, kg, ones2, seq)
        ret_o = _retention(rq, rk, rv, rg, ret_tables, ret_g[l].astype(F32)[None, :], ones2, batch, seq)
        att_o = _attention(aq, ak, av, att_tbl, batch, seq)
        x2 = _out_ffn(x2, hc, ret_o, att_o, w_out[l].astype(BF16), norm2_g[l].astype(F32)[None, :],
                      w_ff1[l].astype(BF16), w_ff2[l].astype(BF16), conv_w[l].astype(F32),
                      conv_b[l].astype(F32)[None, :], conv_g[l].astype(F32)[None, :], ones2, seq)
    return x2.reshape(batch, seq, D_MODEL)
```
